```python
import jax, jax.numpy as jnp
from jax import lax
import numpy as np

D_MODEL = 1024
BATCH = 4
SEQ = 4096
DEPTH = 4

HEAD_DIM = D_MODEL // 16
SGU_HEADS = 4
CONV_GROUPS = 4
ATTN_HEADS = 8
SGU_WIDTH = SGU_HEADS * HEAD_DIM
CONV_WIDTH = CONV_GROUPS * HEAD_DIM
ATTN_WIDTH = ATTN_HEADS * HEAD_DIM
MIX_WIDTH = SGU_WIDTH + CONV_WIDTH + ATTN_WIDTH
IN_WIDTH = 2 * SGU_WIDTH + 2 * CONV_WIDTH + 3 * ATTN_WIDTH
SGU_CHUNK = 128
CONV_KERNEL = 31
DILATED_CONFIGS = ((128, 1), (512, 4), (2048, 16))
ATTN_BLOCK = 128
ALIBI_MAX = 8.0
N_EXPERTS = 256
TOP_K = 8
N_GROUPS = 8
TOP_GROUPS = 4
EXPERT_FF = D_MODEL // 4
SHARED_FF = D_MODEL // 4
ROUTE_SCALE = 2.5
MOE_BLOCK = 128
ADA_CHUNKS = 6
EPS = 1e-6

kernel_name = "hymba_style_sgu_conformer_dilated_moe_trunk"


def rms_norm(x, g):
    x32 = x.astype(jnp.float32)
    y = x32 * lax.rsqrt(jnp.mean(x32 * x32, axis=-1, keepdims=True) + EPS)
    return (y * g.astype(jnp.float32)).astype(x.dtype)


def layer_norm(x, g, b):
    x32 = x.astype(jnp.float32)
    mu = jnp.mean(x32, axis=-1, keepdims=True)
    var = jnp.mean(jnp.square(x32 - mu), axis=-1, keepdims=True)
    y = (x32 - mu) * lax.rsqrt(var + EPS) * g.astype(jnp.float32) + b.astype(jnp.float32)
    return y.astype(x.dtype)


def sgu_mixer(z, norm_g, w_s, b_s):
    B, S, _ = z.shape
    z = jax.nn.gelu(z, approximate=False)
    u, v = jnp.split(z, 2, axis=-1)
    v = rms_norm(v, norm_g)
    v = v.reshape(B, S // SGU_CHUNK, SGU_CHUNK, SGU_HEADS, HEAD_DIM)
    causal = jnp.tril(jnp.ones((SGU_CHUNK, SGU_CHUNK), dtype=bool))
    w = jnp.where(causal[None], w_s, 0.0).astype(v.dtype)
    mixed = jnp.einsum('hts,bnshd->bnthd', w, v) + b_s.T[None, None, :, :, None].astype(v.dtype)
    return u * mixed.reshape(B, S, SGU_WIDTH)


def conv_mixer(z, conv_w, conv_b, ln_g, ln_b):
    a, g = jnp.split(z, 2, axis=-1)
    y = a * jax.nn.sigmoid(g)
    y = lax.conv_general_dilated(
        y, conv_w[:, None, :].astype(y.dtype), window_strides=(1,),
        padding=[(CONV_KERNEL - 1, 0)],
        dimension_numbers=('NWC', 'WIO', 'NWC'),
        feature_group_count=CONV_WIDTH) + conv_b.astype(y.dtype)
    y = layer_norm(y, ln_g, ln_b)
    return jax.nn.silu(y)


def dilated_branch(q, k, v, slopes, window, dilation):
    B, S, H, Dh = q.shape
    r = dilation
    steps = window // r
    blk = ATTN_BLOCK
    L = S // r
    nb = -(-L // blk)
    Lp = nb * blk

    def to_sub(t):
        t = t.reshape(B, L, r, H, Dh).transpose(0, 2, 1, 3, 4).reshape(B * r, L, H, Dh)
        return jnp.pad(t, ((0, 0), (0, Lp - L), (0, 0), (0, 0)))

    def ctx(t):
        tp = jnp.pad(t, ((0, 0), (blk, 0), (0, 0), (0, 0)))
        prev = tp[:, :Lp].reshape(B * r, nb, blk, H, Dh)
        cur = tp[:, blk:].reshape(B * r, nb, blk, H, Dh)
        return jnp.concatenate([prev, cur], axis=2)

    qb = to_sub(q).reshape(B * r, nb, blk, H, Dh)
    kc = ctx(to_sub(k))
    vc = ctx(to_sub(v))
    s = jnp.einsum('bnqhd,bnkhd->bnhqk', qb, kc,
                   preferred_element_type=jnp.float32) * (Dh ** -0.5)
    qi = jnp.arange(blk)[:, None] + blk
    kj = jnp.arange(2 * blk)[None, :]
    dist = qi - kj
    band = (dist >= 0) & (dist <= steps)
    has_prev = (jnp.arange(nb)[:, None, None] > 0) | (kj >= blk)[None]
    valid = band[None] & has_prev
    bias = -slopes[:, None, None] * (dist * r).astype(jnp.float32)[None]
    s = jnp.where(valid[None, :, None], s + bias[None, None], -jnp.inf)
    m = jnp.max(s, axis=-1, keepdims=True)
    p = jnp.exp(s - m)
    den = jnp.sum(p, axis=-1, keepdims=True)
    o = jnp.einsum('bnhqk,bnkhd->bnqhd', p.astype(vc.dtype), vc,
                   preferred_element_type=jnp.float32) / jnp.swapaxes(den, 2, 3)
    lse = jnp.swapaxes((m + jnp.log(den))[..., 0], 2, 3)

    def from_sub(t):
        t = t.reshape(B, r, Lp, *t.shape[3:])[:, :, :L]
        t = jnp.moveaxis(t, 1, 2)
        return t.reshape(B, S, *t.shape[3:])

    return from_sub(o), from_sub(lse)


def dilated_attention(q, k, v, slopes):
    B, S, H, Dh = q.shape
    outs, lses = zip(*[dilated_branch(q, k, v, slopes, w, r) for (w, r) in DILATED_CONFIGS])
    weights = jax.nn.softmax(jnp.stack(lses), axis=0)
    o = jnp.einsum('gbsh,gbshd->bshd', weights, jnp.stack(outs))
    return o.reshape(B, S, ATTN_WIDTH).astype(q.dtype)


def mixer_block(h, slopes, w_in, sgu_g, sgu_w, sgu_b, conv_w, conv_b, ln_g, ln_b, mix_g, w_out):
    B, S, _ = h.shape
    proj = h @ w_in
    cuts = [2 * SGU_WIDTH, 2 * SGU_WIDTH + 2 * CONV_WIDTH,
            2 * SGU_WIDTH + 2 * CONV_WIDTH + ATTN_WIDTH,
            2 * SGU_WIDTH + 2 * CONV_WIDTH + 2 * ATTN_WIDTH]
    z_a, z_b, q, k, v = jnp.split(proj, cuts, axis=-1)
    q = q.reshape(B, S, ATTN_HEADS, HEAD_DIM)
    k = k.reshape(B, S, ATTN_HEADS, HEAD_DIM)
    v = v.reshape(B, S, ATTN_HEADS, HEAD_DIM)
    y_a = sgu_mixer(z_a, sgu_g, sgu_w, sgu_b)
    y_b = conv_mixer(z_b, conv_w, conv_b, ln_g, ln_b)
    y_c = dilated_attention(q, k, v, slopes)
    g_a, g_b, g_c = jnp.split(mix_g, [SGU_WIDTH, SGU_WIDTH + CONV_WIDTH])
    y = jnp.concatenate([rms_norm(y_a, g_a), rms_norm(y_b, g_b), rms_norm(y_c, g_c)], axis=-1)
    return y @ w_out


def moe_ffn(h, router_w, router_bias, w1, w3, w2, sw1, sw3, sw2):
    B, S, D = h.shape
    T = B * S
    x = h.reshape(T, D)
    scores = jax.nn.sigmoid(jnp.dot(x, router_w, preferred_element_type=jnp.float32))
    biased = scores + router_bias.astype(jnp.float32)
    grouped = biased.reshape(T, N_GROUPS, N_EXPERTS // N_GROUPS)
    group_score = jnp.sum(lax.top_k(grouped, 2)[0], axis=-1)
    _, top_groups = lax.top_k(group_score, TOP_GROUPS)
    group_mask = jnp.any(top_groups[:, :, None] == jnp.arange(N_GROUPS)[None, None, :], axis=1)
    expert_mask = jnp.repeat(group_mask, N_EXPERTS // N_GROUPS, axis=1)
    _, idx = lax.top_k(jnp.where(expert_mask, biased, -jnp.inf), TOP_K)
    gate = jnp.take_along_axis(scores, idx, axis=-1)
    gate = gate / jnp.sum(gate, axis=-1, keepdims=True) * ROUTE_SCALE

    A = T * TOP_K
    e_flat = idx.reshape(A)
    tok_flat = jnp.repeat(jnp.arange(T, dtype=jnp.int32), TOP_K)
    order = jnp.argsort(e_flat)
    se, stok, sg = e_flat[order], tok_flat[order], gate.reshape(A)[order]
    counts = jnp.zeros((N_EXPERTS,), jnp.int32).at[e_flat].add(1)
    padded = (counts + MOE_BLOCK - 1) // MOE_BLOCK * MOE_BLOCK
    pend = jnp.cumsum(padded)
    pstart = pend - padded
    start = jnp.cumsum(counts) - counts
    dest = pstart[se] + jnp.arange(A, dtype=jnp.int32) - start[se]
    n_blocks = (A + N_EXPERTS * (MOE_BLOCK - 1) + MOE_BLOCK - 1) // MOE_BLOCK
    n_rows = n_blocks * MOE_BLOCK
    row_tok = jnp.zeros((n_rows,), jnp.int32).at[dest].set(stok)
    row_gate = jnp.zeros((n_rows,), jnp.float32).at[dest].set(sg)
    block_e = jnp.minimum(jnp.searchsorted(pend, jnp.arange(n_blocks, dtype=jnp.int32) * MOE_BLOCK,
                                           side='right'), N_EXPERTS - 1)
    xd = x[row_tok].reshape(n_blocks, MOE_BLOCK, D)

    def expert_block(args):
        xb, e = args
        hid = jax.nn.silu(xb @ w1[e]) * (xb @ w3[e])
        return hid @ w2[e]

    yd = lax.map(expert_block, (xd, block_e)).reshape(n_rows, D)
    routed = jnp.zeros((T, D), jnp.float32).at[row_tok].add(yd.astype(jnp.float32) * row_gate[:, None])
    shared = (jax.nn.silu(x @ sw1) * (x @ sw3)) @ sw2
    return (routed.astype(x.dtype) + shared).reshape(B, S, D)


def setup_inputs(seed: int = 0) -> dict:
    key = jax.random.key(seed)
    ks = jax.random.split(key, 26)
    f32 = jnp.float32

    def nrm(k, shape, scale):
        return jax.random.normal(k, shape, f32) * scale

    def gain(k, shape):
        return 1.0 + 0.02 * jax.random.normal(k, shape, f32)

    L, D = DEPTH, D_MODEL
    return {
        "x": nrm(ks[0], (BATCH, SEQ, D), 1.0),
        "c": nrm(ks[1], (BATCH, D), 1.0),
        "norm1_g": gain(ks[2], (L, D)),
        "norm2_g": gain(ks[3], (L, D)),
        "ada_w": nrm(ks[4], (L, D, ADA_CHUNKS * D), 0.3 * D ** -0.5),
        "ada_b": nrm(ks[5], (L, ADA_CHUNKS * D), 0.02),
        "w_in": nrm(ks[6], (L, D, IN_WIDTH), D ** -0.5),
        "sgu_norm_g": gain(ks[7], (L, SGU_WIDTH)),
        "sgu_w": nrm(ks[8], (L, SGU_HEADS, SGU_CHUNK, SGU_CHUNK), SGU_CHUNK ** -0.5),
        "sgu_b": 1.0 + nrm(ks[9], (L, SGU_HEADS, SGU_CHUNK), 0.1),
        "conv_w": nrm(ks[10], (L, CONV_KERNEL, CONV_WIDTH), CONV_KERNEL ** -0.5),
        "conv_b": nrm(ks[11], (L, CONV_WIDTH), 0.02),
        "conv_ln_g": gain(ks[12], (L, CONV_WIDTH)),
        "conv_ln_b": nrm(ks[13], (L, CONV_WIDTH), 0.02),
        "mix_norm_g": gain(ks[14], (L, MIX_WIDTH)),
        "w_out": nrm(ks[15], (L, MIX_WIDTH, D), MIX_WIDTH ** -0.5),
        "router_w": nrm(ks[16], (L, D, N_EXPERTS), D ** -0.5),
        "router_bias": nrm(ks[17], (L, N_EXPERTS), 0.01),
        "expert_w1": nrm(ks[18], (L, N_EXPERTS, D, EXPERT_FF), D ** -0.5),
        "expert_w3": nrm(ks[19], (L, N_EXPERTS, D, EXPERT_FF), D ** -0.5),
        "expert_w2": nrm(ks[20], (L, N_EXPERTS, EXPERT_FF, D), EXPERT_FF ** -0.5),
        "shared_w1": nrm(ks[21], (L, D, SHARED_FF), D ** -0.5),
        "shared_w3": nrm(ks[22], (L, D, SHARED_FF), D ** -0.5),
        "shared_w2": nrm(ks[23], (L, SHARED_FF, D), SHARED_FF ** -0.5),
        "final_norm_g": gain(ks[24], (D,)),
    }


def reference(x, c, norm1_g, norm2_g, ada_w, ada_b, w_in, sgu_norm_g, sgu_w, sgu_b,
              conv_w, conv_b, conv_ln_g, conv_ln_b, mix_norm_g, w_out, router_w, router_bias,
              expert_w1, expert_w3, expert_w2, shared_w1, shared_w3, shared_w2, final_norm_g):
    slopes = jnp.exp2(-ALIBI_MAX * jnp.arange(1, ATTN_HEADS + 1, dtype=jnp.float32) / ATTN_HEADS)
    cond = jax.nn.silu(c)
    for l in range(DEPTH):
        mod = cond @ ada_w[l] + ada_b[l]
        shift1, scale1, gate1, shift2, scale2, gate2 = jnp.split(mod[:, None, :], ADA_CHUNKS, axis=-1)
        h = rms_norm(x, norm1_g[l]) * (1.0 + scale1) + shift1
        x = x + gate1 * mixer_block(h, slopes, w_in[l], sgu_norm_g[l], sgu_w[l], sgu_b[l],
                                    conv_w[l], conv_b[l], conv_ln_g[l], conv_ln_b[l],
                                    mix_norm_g[l], w_out[l])
        h = rms_norm(x, norm2_g[l]) * (1.0 + scale2) + shift2
        x = x + gate2 * moe_ffn(h, router_w[l], router_bias[l], expert_w1[l], expert_w3[l],
                                expert_w2[l], shared_w1[l], shared_w3[l], shared_w2[l])
    return rms_norm(x, final_norm_g)
```

```python
import functools

import jax
import jax.numpy as jnp
from jax import lax
from jax.experimental import pallas as pl
from jax.experimental.pallas import tpu as pltpu

F32 = jnp.float32
BF16 = jnp.bfloat16
I32 = jnp.int32

D_MODEL = 1024
HEAD_DIM = 64
SGU_WIDTH = 256
CONV_WIDTH = 256
ATTN_WIDTH = 512
ATTN_HEADS = 8
SGU_HEADS = 4
SGU_CHUNK = 128
CONV_KERNEL = 31
CONV_HALO = 32
DILATIONS = (1, 4, 16)
ATTN_BLOCK = 128
ALIBI_MAX = 8.0
N_EXPERTS = 256
N_GROUPS = 8
GROUP_SIZE = N_EXPERTS // N_GROUPS
TOP_K = 8
TOP_GROUPS = 4
EXPERT_FF = 256
ROUTE_SCALE = 2.5
ROW_BLOCK = 128
ADA_CHUNKS = 6
EPS = 1e-6
LANES = 128
SUBLANES = 8
ROW_TILES = D_MODEL // LANES
NEG_BIG = -1e30
VMEM_LIMIT = 56 * 1024 * 1024


def _sigmoid(x):
    return 1.0 / (1.0 + jnp.exp(-x))


def _silu(x):
    return x * _sigmoid(x)


def _params(semantics):
    return pltpu.CompilerParams(dimension_semantics=semantics, vmem_limit_bytes=VMEM_LIMIT)


def _ada_body(c_ref, w_ref, b_ref, o_ref):
    c = c_ref[...]
    cond = _silu(c).astype(BF16)
    o_ref[...] = jnp.dot(cond, w_ref[...].astype(BF16), preferred_element_type=F32) + b_ref[...]


def _ada_mod(c_pad, ada_w, ada_b, tn=1536):
    depth, d, n = ada_w.shape
    rows = c_pad.shape[0]
    return pl.pallas_call(
        _ada_body,
        grid=(depth, n // tn),
        in_specs=[
            pl.BlockSpec((rows, d), lambda l, j: (0, 0)),
            pl.BlockSpec((None, d, tn), lambda l, j: (l, 0, j)),
            pl.BlockSpec((None, 1, tn), lambda l, j: (l, 0, j)),
        ],
        out_specs=pl.BlockSpec((None, rows, tn), lambda l, j: (l, 0, j)),
        out_shape=jax.ShapeDtypeStruct((depth, rows, n), F32),
        compiler_params=_params(("parallel", "parallel")),
        name="ada_mod",
    )(c_pad, ada_w, ada_b.reshape(depth, 1, n))


def _inproj_body(x_ref, mod_ref, g_ref, w_ref, za_ref, zb_ref, q_ref, k_ref, v_ref):
    x = x_ref[...]
    ms = jnp.mean(x * x, axis=-1, keepdims=True)
    h = x * lax.rsqrt(ms + EPS) * g_ref[...]
    h = h * (1.0 + mod_ref[1:2, :]) + mod_ref[0:1, :]
    hb = h.astype(BF16)
    outs = (za_ref, zb_ref, q_ref, k_ref, v_ref)
    for j, o_ref in enumerate(outs):
        r = jnp.dot(hb, w_ref[:, j * 512:(j + 1) * 512], preferred_element_type=F32)
        if j == 2:
            r = r * (HEAD_DIM ** -0.5)
        o_ref[...] = r.astype(o_ref.dtype)


def _inproj(x2d, mod_l, g, w_in_b, seq, tm=512):
    t, d = x2d.shape
    out = jax.ShapeDtypeStruct((t, 512), BF16)
    ospec = pl.BlockSpec((tm, 512), lambda i: (i, 0))
    return pl.pallas_call(
        _inproj_body,
        grid=(t // tm,),
        in_specs=[
            pl.BlockSpec((tm, d), lambda i: (i, 0)),
            pl.BlockSpec((None, ADA_CHUNKS, d), lambda i: ((i * tm) // seq, 0, 0)),
            pl.BlockSpec((1, d), lambda i: (0, 0)),
            pl.BlockSpec(w_in_b.shape, lambda i: (0, 0)),
        ],
        out_specs=[ospec] * 5,
        out_shape=[out] * 5,
        compiler_params=_params(("parallel",)),
        name="inproj",
    )(x2d, mod_l, g.reshape(1, d), w_in_b)


def _sguconv_body(za_ref, zb_ref, zbh_ref, sgug_ref, wcat_ref, bmat_ref, cw_ref, cb_ref,
                  lng_ref, lnb_ref, ga_ref, gb_ref, o_ref, ybuf, *, ts):
    i = pl.program_id(1)
    z = za_ref[...].astype(F32)
    z = 0.5 * z * (1.0 + lax.erf(z * (2.0 ** -0.5)))
    u = z[:, :SGU_WIDTH]
    v = z[:, SGU_WIDTH:]
    v = v * lax.rsqrt(jnp.mean(v * v, axis=-1, keepdims=True) + EPS) * sgug_ref[...]
    vb = v.astype(BF16)
    wt = lax.broadcasted_iota(I32, (SGU_CHUNK, SGU_HEADS * SGU_CHUNK), 0)
    ws = lax.broadcasted_iota(I32, (SGU_CHUNK, SGU_HEADS * SGU_CHUNK), 1) % SGU_CHUNK
    wb = jnp.where(ws <= wt, wcat_ref[...], 0.0).astype(BF16)
    head_of_lane = lax.broadcasted_iota(I32, (1, SGU_WIDTH), 1) // HEAD_DIM
    for c in range(ts // SGU_CHUNK):
        rows = slice(c * SGU_CHUNK, (c + 1) * SGU_CHUNK)
        vc = vb[rows, :]
        rhs = jnp.concatenate(
            [jnp.where(head_of_lane == h, vc, jnp.zeros_like(vc)) for h in range(SGU_HEADS)], axis=0)
        mixed = jnp.dot(wb, rhs, preferred_element_type=F32) + bmat_ref[...]
        ya = u[rows, :] * mixed
        ya = ya * lax.rsqrt(jnp.mean(ya * ya, axis=-1, keepdims=True) + EPS) * ga_ref[...]
        o_ref[rows, 0:SGU_WIDTH] = ya.astype(o_ref.dtype)
    zh = zbh_ref[...].astype(F32)
    yh = zh[:, :CONV_WIDTH] * _sigmoid(zh[:, CONV_WIDTH:])
    ybuf[0:CONV_HALO, :] = jnp.where(i == 0, 0.0, yh)
    zc = zb_ref[...].astype(F32)
    ybuf[CONV_HALO:CONV_HALO + ts, :] = zc[:, :CONV_WIDTH] * _sigmoid(zc[:, CONV_WIDTH:])
    rb = 64
    first_tap = CONV_HALO - (CONV_KERNEL - 1)
    for blk in range(ts // rb):
        acc = jnp.zeros((rb, CONV_WIDTH), F32) + cb_ref[...]
        for kk in range(CONV_KERNEL):
            start = blk * rb + first_tap + kk
            acc = acc + cw_ref[kk:kk + 1, :] * ybuf[start:start + rb, :]
        mu = jnp.mean(acc, axis=-1, keepdims=True)
        cen = acc - mu
        var = jnp.mean(cen * cen, axis=-1, keepdims=True)
        y = cen * lax.rsqrt(var + EPS) * lng_ref[...] + lnb_ref[...]
        y = _silu(y)
        y = y * lax.rsqrt(jnp.mean(y * y, axis=-1, keepdims=True) + EPS) * gb_ref[...]
        o_ref[blk * rb:(blk + 1) * rb, SGU_WIDTH:SGU_WIDTH + CONV_WIDTH] = y.astype(o_ref.dtype)


def _sguconv(za, zb, sgu_g, wcat, bmat, conv_w, conv_b, ln_g, ln_b, g_a, g_b, batch, seq, ts=512):
    t = za.shape[0]
    nt = seq // ts
    hpt = ts // CONV_HALO
    row = lambda a: a.reshape(1, -1)
    full = lambda a: pl.BlockSpec(a.shape, lambda b, i: (0,) * a.ndim)
    args = (row(sgu_g), wcat, bmat, conv_w, row(conv_b), row(ln_g), row(ln_b), row(g_a), row(g_b))
    return pl.pallas_call(
        functools.partial(_sguconv_body, ts=ts),
        grid=(batch, nt),
        in_specs=[
            pl.BlockSpec((ts, 512), lambda b, i: (b * nt + i, 0)),
            pl.BlockSpec((ts, 512), lambda b, i: (b * nt + i, 0)),
            pl.BlockSpec((CONV_HALO, 512), lambda b, i: (jnp.maximum((b * nt + i) * hpt - 1, 0), 0)),
        ] + [full(a) for a in args],
        out_specs=pl.BlockSpec((ts, 512), lambda b, i: (b * nt + i, 0)),
        out_shape=jax.ShapeDtypeStruct((t, 512), BF16),
        scratch_shapes=[pltpu.VMEM((CONV_HALO + ts, CONV_WIDTH), F32)],
        compiler_params=_params(("parallel", "arbitrary")),
        name="sguconv",
    )(za, zb, zb, *args)


def _attn_body(slopes_ref, q_ref, k_ref, v_ref, o_ref, lse_ref, bias_ref, *, r, nb, hpb):
    hp0 = pl.program_id(2) * hpb
    qi = lax.broadcasted_iota(I32, (ATTN_BLOCK, ATTN_BLOCK), 0)
    kj = lax.broadcasted_iota(I32, (ATTN_BLOCK, ATTN_BLOCK), 1)
    d_cur = qi - kj
    d_prev = d_cur + ATTN_BLOCK
    lane = lax.broadcasted_iota(I32, (1, LANES), 1)
    low = lane < HEAD_DIM
    dn = (((1,), (1,)), ((), ()))

    for p in range(hpb):
        lanes = slice(p * LANES, (p + 1) * LANES)
        for hh in range(2):
            slope = slopes_ref[(hp0 + p) * 2 + hh]
            bias_ref[2 * hh] = jnp.where(d_cur >= 0, -slope * (d_cur * r).astype(F32), NEG_BIG)
            bias_ref[2 * hh + 1] = jnp.where(d_cur <= 0, -slope * (d_prev * r).astype(F32), NEG_BIG)

        def block(n, has_prev, lanes=lanes):
            row0 = pl.multiple_of(n * ATTN_BLOCK, ATTN_BLOCK)
            cur = pl.ds(row0, ATTN_BLOCK)
            qn = q_ref[cur, lanes]
            kn = k_ref[cur, lanes]
            vn = v_ref[cur, lanes]
            if has_prev:
                prv = pl.ds(pl.multiple_of(row0 - ATTN_BLOCK, ATTN_BLOCK), ATTN_BLOCK)
                kp = k_ref[prv, lanes]
                vp = v_ref[prv, lanes]
            outs, lses = [], []
            for hh in range(2):
                mine = low if hh == 0 else jnp.logical_not(low)
                qm = jnp.where(mine, qn, jnp.zeros_like(qn))
                s_c = lax.dot_general(qm, kn, dn, preferred_element_type=F32) + bias_ref[2 * hh]
                m = jnp.max(s_c, axis=-1, keepdims=True)
                if has_prev:
                    s_p = lax.dot_general(qm, kp, dn, preferred_element_type=F32) + bias_ref[2 * hh + 1]
                    m = jnp.maximum(m, jnp.max(s_p, axis=-1, keepdims=True))
                p_c = jnp.exp(s_c - m)
                den = jnp.sum(p_c, axis=-1, keepdims=True)
                acc = jnp.dot(p_c.astype(BF16), vn, preferred_element_type=F32)
                if has_prev:
                    p_p = jnp.exp(s_p - m)
                    den = den + jnp.sum(p_p, axis=-1, keepdims=True)
                    acc = acc + jnp.dot(p_p.astype(BF16), vp, preferred_element_type=F32)
                outs.append(acc / den)
                lses.append(m + jnp.log(den))
            o_ref[cur, lanes] = jnp.where(low, outs[0], outs[1]).astype(o_ref.dtype)
            lse_ref[cur, lanes] = jnp.where(low, lses[0], lses[1])

        block(0, False)

        def step(n, carry):
            block(n, True)
            return carry

        lax.fori_loop(1, nb, step, 0)


def _attn_branch(slopes, q, k, v, batch, seq, r):
    length = seq // r
    nb = length // ATTN_BLOCK
    hpb = 1 if r == 1 else 4
    width = hpb * LANES
    npb = ATTN_WIDTH // width
    view = lambda a: a.reshape(batch, length, r * ATTN_WIDTH)
    spec = pl.BlockSpec((None, length, width), lambda b, rho, hp, *_: (b, 0, rho * npb + hp))
    o, lse = pl.pallas_call(
        functools.partial(_attn_body, r=r, nb=nb, hpb=hpb),
        grid_spec=pltpu.PrefetchScalarGridSpec(
            num_scalar_prefetch=1,
            grid=(batch, r, npb),
            in_specs=[spec, spec, spec],
            out_specs=[spec, spec],
            scratch_shapes=[pltpu.VMEM((4, ATTN_BLOCK, ATTN_BLOCK), F32)],
        ),
        out_shape=[jax.ShapeDtypeStruct((batch, length, r * ATTN_WIDTH), BF16),
                   jax.ShapeDtypeStruct((batch, length, r * ATTN_WIDTH), F32)],
        compiler_params=_params(("parallel", "parallel", "parallel")),
        name=f"attn_r{r}",
    )(slopes, view(q), view(k), view(v))
    return o.reshape(batch * seq, ATTN_WIDTH), lse.reshape(batch * seq, ATTN_WIDTH)


def _route_chunk(logits, rbias, run, upper, ones):
    nj = GROUP_SIZE
    gidx = lax.broadcasted_iota(I32, (N_GROUPS, LANES), 0)
    sc = [_sigmoid(logits[j * 8:(j + 1) * 8, :]) for j in range(nj)]
    bs = [sc[j] + rbias[j * 8:(j + 1) * 8, :] for j in range(nj)]
    tree = lambda f, xs: functools.reduce(f, xs)
    neg_inf = -jnp.inf
    m1 = tree(jnp.maximum, bs)
    first = tree(jnp.minimum, [jnp.where(bs[j] == m1, j, nj) for j in range(nj)])
    m2 = tree(jnp.maximum, [jnp.where(first == j, neg_inf, bs[j]) for j in range(nj)])
    gs = m1 + m2
    beaten = jnp.zeros((N_GROUPS, LANES), I32)
    for g2 in range(N_GROUPS):
        other = gs[g2:g2 + 1, :]
        beats = (other > gs) | ((other == gs) & (g2 < gidx))
        beaten = beaten + beats.astype(I32)
    gsel = beaten < TOP_GROUPS
    mk = [jnp.where(gsel, bs[j], neg_inf) for j in range(nj)]
    eid = [gidx * GROUP_SIZE + j for j in range(nj)]
    selm = [jnp.zeros((N_GROUPS, LANES), F32) for _ in range(nj)]
    idx_rows, gate_rows = [], []
    for _ in range(TOP_K):
        mx = jnp.max(tree(jnp.maximum, mk), axis=0, keepdims=True)
        cand = tree(jnp.minimum, [jnp.where(mk[j] == mx, eid[j], N_EXPERTS) for j in range(nj)])
        sel = jnp.min(cand, axis=0, keepdims=True)
        hit = [eid[j] == sel for j in range(nj)]
        gk = jnp.sum(tree(jnp.add, [jnp.where(hit[j], sc[j], 0.0) for j in range(nj)]), axis=0, keepdims=True)
        mk = [jnp.where(hit[j], neg_inf, mk[j]) for j in range(nj)]
        selm = [selm[j] + hit[j].astype(F32) for j in range(nj)]
        idx_rows.append(sel)
        gate_rows.append(gk)
    gsum = tree(jnp.add, gate_rows)
    gate = jnp.concatenate([g / gsum * ROUTE_SCALE for g in gate_rows], axis=0)
    idx = jnp.concatenate(idx_rows, axis=0)
    msel = jnp.concatenate(selm, axis=0).astype(BF16)
    before = run + jnp.dot(msel, upper, preferred_element_type=F32)
    rank_rows = []
    for kk in range(TOP_K):
        contrib = [jnp.where(eid[j] == idx_rows[kk], before[j * 8:(j + 1) * 8, :], 0.0) for j in range(nj)]
        rank_rows.append(jnp.sum(tree(jnp.add, contrib), axis=0, keepdims=True))
    rank = jnp.concatenate(rank_rows, axis=0).astype(I32)
    run = run + jnp.dot(msel, ones, preferred_element_type=F32)
    return idx, gate, rank, run


def _post_body(yab_ref, o1_ref, o2_ref, o3_ref, l1_ref, l2_ref, l3_ref, x_ref, mod_ref, gc_ref, wout_ref,
               n2g_ref, rwh_ref, rwl_ref, rb_ref, sw1_ref, sw3_ref, sw2_ref,
               x2_ref, h2_ref, idx_ref, gate_ref, rank_ref, cnt_ref, run_ref, *, tm):
    i = pl.program_id(0)

    @pl.when(i == 0)
    def _():
        run_ref[...] = jnp.zeros_like(run_ref)

    l1, l2, l3 = l1_ref[...], l2_ref[...], l3_ref[...]
    mx = jnp.maximum(jnp.maximum(l1, l2), l3)
    e1, e2, e3 = jnp.exp(l1 - mx), jnp.exp(l2 - mx), jnp.exp(l3 - mx)
    yc = (e1 * o1_ref[...].astype(F32) + e2 * o2_ref[...].astype(F32) + e3 * o3_ref[...].astype(F32)) / (e1 + e2 + e3)
    yc = yc * lax.rsqrt(jnp.mean(yc * yc, axis=-1, keepdims=True) + EPS) * gc_ref[...]
    nab = SGU_WIDTH + CONV_WIDTH
    mix = (jnp.dot(yab_ref[...], wout_ref[0:nab, :], preferred_element_type=F32)
           + jnp.dot(yc.astype(BF16), wout_ref[nab:, :], preferred_element_type=F32))
    x1 = x_ref[...] + mod_ref[2:3, :] * mix
    h2 = x1 * lax.rsqrt(jnp.mean(x1 * x1, axis=-1, keepdims=True) + EPS) * n2g_ref[...]
    h2 = h2 * (1.0 + mod_ref[4:5, :]) + mod_ref[3:4, :]
    for c in range(ROW_TILES):
        h2_ref[pl.ds(c, tm, stride=ROW_TILES), :] = h2[:, c * LANES:(c + 1) * LANES]
    hb = h2.astype(BF16)
    a = jnp.dot(hb, sw1_ref[...], preferred_element_type=F32)
    b = jnp.dot(hb, sw3_ref[...], preferred_element_type=F32)
    sh = jnp.dot((_silu(a) * b).astype(BF16), sw2_ref[...], preferred_element_type=F32)
    x2_ref[...] = x1 + mod_ref[5:6, :] * sh
    hl = (h2 - hb.astype(F32)).astype(BF16)
    dn = (((1,), (1,)), ((), ()))
    logits = (lax.dot_general(rwh_ref[...], hb, dn, preferred_element_type=F32)
              + lax.dot_general(rwl_ref[...], hb, dn, preferred_element_type=F32)
              + lax.dot_general(rwh_ref[...], hl, dn, preferred_element_type=F32))
    tt = lax.broadcasted_iota(I32, (LANES, LANES), 0)
    tc = lax.broadcasted_iota(I32, (LANES, LANES), 1)
    upper = (tt < tc).astype(BF16)
    ones = jnp.ones((LANES, LANES), BF16)
    run = run_ref[...]
    for c in range(tm // LANES):
        cols = slice(c * LANES, (c + 1) * LANES)
        idx, gate, rank, run = _route_chunk(logits[:, cols], rb_ref[...], run, upper, ones)
        idx_ref[:, cols] = idx
        gate_ref[:, cols] = gate
        rank_ref[:, cols] = rank
    run_ref[...] = run
    cnt_ref[...] = run


def _post(yab, os_, ls_, x2d, mod_l, g_c, wout_b, n2g, rwh, rwl, rb, sw1, sw3, sw2, seq, tm=256):
    t, d = x2d.shape
    row = lambda a: a.reshape(1, -1)
    tok = lambda w: pl.BlockSpec((tm, w), lambda i: (i, 0))
    full = lambda a: pl.BlockSpec(a.shape, lambda i: (0,) * a.ndim)
    consts = (row(g_c), wout_b, row(n2g), rwh, rwl, rb, sw1, sw3, sw2)
    kt = lambda dt: jax.ShapeDtypeStruct((TOP_K, t), dt)
    kspec = pl.BlockSpec((TOP_K, tm), lambda i: (0, i))
    return pl.pallas_call(
        functools.partial(_post_body, tm=tm),
        grid=(t // tm,),
        in_specs=[tok(512)] * 7 + [tok(d), pl.BlockSpec((None, ADA_CHUNKS, d), lambda i: ((i * tm) // seq, 0, 0))]
        + [full(a) for a in consts],
        out_specs=[tok(d), pl.BlockSpec((tm * ROW_TILES, LANES), lambda i: (i, 0)), kspec, kspec, kspec,
                   pl.BlockSpec((N_EXPERTS, LANES), lambda i: (0, 0))],
        out_shape=[jax.ShapeDtypeStruct((t, d), F32), jax.ShapeDtypeStruct((t * ROW_TILES, LANES), F32),
                   kt(I32), kt(F32), kt(I32), jax.ShapeDtypeStruct((N_EXPERTS, LANES), F32)],
        scratch_shapes=[pltpu.VMEM((N_EXPERTS, LANES), F32)],
        compiler_params=_params(("arbitrary",)),
        name="post",
    )(yab, *os_, *ls_, x2d, mod_l, *consts)


_PAD_BITS = (64, 32, 16, 8, 4, 2, 1)


def _row_slab(ref, row):
    return ref.at[pl.ds(pl.multiple_of(row * ROW_TILES, ROW_TILES), ROW_TILES), :]


def _dispatch_body(pstart_ref, cnt_ref, idx_ref, rank_ref, h_ref, xd_ref, zbuf, sem, zsem, *, tq):
    i = pl.program_id(0)

    @pl.when(i == 0)
    def _():
        zbuf[...] = jnp.zeros_like(zbuf)

        def pad_copies(e, act):
            base = pstart_ref[e] + cnt_ref[e]
            npad = (-cnt_ref[e]) & (ROW_BLOCK - 1)
            for bit in _PAD_BITS:
                nrows = bit * ROW_TILES

                @pl.when((npad & bit) != 0)
                def _(base=base, nrows=nrows):
                    dst = xd_ref.at[pl.ds(pl.multiple_of(base * ROW_TILES, ROW_TILES), nrows), :]
                    act(pltpu.make_async_copy(zbuf.at[pl.ds(0, nrows), :], dst, zsem))

                base = base + (npad & bit)

        def issue(e, carry):
            pad_copies(e, lambda cp: cp.start())
            return carry

        def drain(e, carry):
            pad_copies(e, lambda cp: cp.wait())
            return carry

        lax.fori_loop(0, N_EXPERTS, issue, 0)
        lax.fori_loop(0, N_EXPERTS, drain, 0)

    def token(t, carry):
        src = _row_slab(h_ref, t)
        for kk in range(TOP_K):
            dest = pstart_ref[idx_ref[kk, t]] + rank_ref[kk, t]
            pltpu.make_async_copy(src, _row_slab(xd_ref, dest), sem).start()
        return carry

    lax.fori_loop(0, tq, token, 0)
    for kk in range(TOP_K):
        pltpu.make_async_copy(h_ref, xd_ref.at[pl.ds(0, tq * ROW_TILES), :], sem).wait()


def _dispatch(pstart, counts, idx, rank, h2rows, n_rows, tq=128):
    t = idx.shape[1]
    smem = pl.BlockSpec((TOP_K, tq), lambda i, *_: (0, i), memory_space=pltpu.SMEM)
    return pl.pallas_call(
        functools.partial(_dispatch_body, tq=tq),
        grid_spec=pltpu.PrefetchScalarGridSpec(
            num_scalar_prefetch=2,
            grid=(t // tq,),
            in_specs=[smem, smem, pl.BlockSpec((tq * ROW_TILES, LANES), lambda i, *_: (i, 0))],
            out_specs=pl.BlockSpec(memory_space=pl.ANY),
            scratch_shapes=[pltpu.VMEM((_PAD_BITS[0] * ROW_TILES, LANES), F32),
                            pltpu.SemaphoreType.DMA, pltpu.SemaphoreType.DMA],
        ),
        out_shape=jax.ShapeDtypeStruct((n_rows * ROW_TILES, LANES), F32),
        compiler_params=_params(("arbitrary",)),
        name="dispatch",
    )(pstart, counts, idx, rank, h2rows)


def _expert_body(be_ref, nu_ref, xd_ref, w1_ref, w3_ref, w2_ref, yd_ref, w1b, w3b, w2b):
    i = pl.program_id(0)
    changed = (i == 0) | (be_ref[i] != be_ref[jnp.maximum(i - 1, 0)])

    @pl.when(changed)
    def _():
        w1b[...] = w1_ref[...].astype(BF16)
        w3b[...] = w3_ref[...].astype(BF16)
        w2b[...] = w2_ref[...].astype(BF16)

    @pl.when(i < nu_ref[0])
    def _():
        x = jnp.concatenate([xd_ref[pl.ds(c, ROW_BLOCK, stride=ROW_TILES), :] for c in range(ROW_TILES)], axis=1)
        xb = x.astype(BF16)
        a = jnp.dot(xb, w1b[...], preferred_element_type=F32)
        b = jnp.dot(xb, w3b[...], preferred_element_type=F32)
        y = jnp.dot((_silu(a) * b).astype(BF16), w2b[...], preferred_element_type=F32)
        for c in range(ROW_TILES):
            yd_ref[pl.ds(c, ROW_BLOCK, stride=ROW_TILES), :] = y[:, c * LANES:(c + 1) * LANES]


def _experts(block_e, n_used, xd, w1, w3, w2, n_blocks):
    d, ff = w1.shape[1], w1.shape[2]
    rows = pl.BlockSpec((ROW_BLOCK * ROW_TILES, LANES), lambda i, be, nu: (jnp.minimum(i, nu[0] - 1), 0))
    return pl.pallas_call(
        _expert_body,
        grid_spec=pltpu.PrefetchScalarGridSpec(
            num_scalar_prefetch=2,
            grid=(n_blocks,),
            in_specs=[rows,
                      pl.BlockSpec((None, d, ff), lambda i, be, nu: (be[i], 0, 0)),
                      pl.BlockSpec((None, d, ff), lambda i, be, nu: (be[i], 0, 0)),
                      pl.BlockSpec((None, ff, d), lambda i, be, nu: (be[i], 0, 0))],
            out_specs=rows,
            scratch_shapes=[pltpu.VMEM((d, ff), BF16), pltpu.VMEM((d, ff), BF16), pltpu.VMEM((ff, d), BF16)],
        ),
        out_shape=jax.ShapeDtypeStruct(xd.shape, F32),
        compiler_params=_params(("arbitrary",)),
        name="experts",
    )(block_e, n_used, xd, w1, w3, w2)


def _combine_body(pstart_ref, idx_ref, rank_ref, gt_ref, x2_ref, mod_ref, fg_ref, yd_ref, o_ref, buf, sem,
                  *, tq, final):
    tile_rows = tq * ROW_TILES

    def token(t, carry):
        for kk in range(TOP_K):
            dest = pstart_ref[idx_ref[kk, t]] + rank_ref[kk, t]
            dst = buf.at[pl.ds(pl.multiple_of(kk * tile_rows + t * ROW_TILES, ROW_TILES), ROW_TILES), :]
            pltpu.make_async_copy(_row_slab(yd_ref, dest), dst, sem).start()
        return carry

    lax.fori_loop(0, tq, token, 0)
    for kk in range(TOP_K):
        pltpu.make_async_copy(yd_ref.at[pl.ds(0, tile_rows), :],
                              buf.at[pl.ds(kk * tile_rows, tile_rows), :], sem).wait()
    gates = gt_ref[...]
    outs = []
    for c in range(ROW_TILES):
        routed = jnp.zeros((tq, LANES), F32)
        for kk in range(TOP_K):
            routed = routed + gates[:, kk:kk + 1] * buf[pl.ds(kk * tile_rows + c, tq, stride=ROW_TILES), :]
        cols = slice(c * LANES, (c + 1) * LANES)
        outs.append(x2_ref[:, cols] + mod_ref[5:6, cols] * routed)
    if final:
        ms = functools.reduce(jnp.add, [jnp.sum(o * o, axis=-1, keepdims=True) for o in outs]) / D_MODEL
        inv = lax.rsqrt(ms + EPS)
        outs = [o * inv * fg_ref[:, c * LANES:(c + 1) * LANES] for c, o in enumerate(outs)]
    for c, o in enumerate(outs):
        o_ref[:, c * LANES:(c + 1) * LANES] = o


def _combine(pstart, idx, rank, gate_t, x2, mod_l, final_g, yd, seq, final, tq=128):
    t, d = x2.shape
    smem = pl.BlockSpec((TOP_K, tq), lambda i, *_: (0, i), memory_space=pltpu.SMEM)
    return pl.pallas_call(
        functools.partial(_combine_body, tq=tq, final=final),
        grid_spec=pltpu.PrefetchScalarGridSpec(
            num_scalar_prefetch=1,
            grid=(t // tq,),
            in_specs=[smem, smem,
                      pl.BlockSpec((tq, TOP_K), lambda i, *_: (i, 0)),
                      pl.BlockSpec((tq, d), lambda i, *_: (i, 0)),
                      pl.BlockSpec((None, ADA_CHUNKS, d), lambda i, *_: ((i * tq) // seq, 0, 0)),
                      pl.BlockSpec((1, d), lambda i, *_: (0, 0)),
                      pl.BlockSpec(memory_space=pl.ANY)],
            out_specs=pl.BlockSpec((tq, d), lambda i, *_: (i, 0)),
            scratch_shapes=[pltpu.VMEM((TOP_K * tq * ROW_TILES, LANES), F32), pltpu.SemaphoreType.DMA],
        ),
        out_shape=jax.ShapeDtypeStruct((t, d), F32),
        compiler_params=_params(("arbitrary",)),
        name="combine",
    )(pstart, idx, rank, gate_t, x2, mod_l, final_g.reshape(1, d), yd)


def _row_layout(counts_perm, n_blocks):
    counts = counts_perm.reshape(GROUP_SIZE, N_GROUPS).T.reshape(N_EXPERTS).astype(I32)
    padded = (counts + ROW_BLOCK - 1) // ROW_BLOCK * ROW_BLOCK
    pend = jnp.cumsum(padded)
    pstart = pend - padded
    block_e = jnp.minimum(
        jnp.searchsorted(pend, jnp.arange(n_blocks, dtype=I32) * ROW_BLOCK, side='right'), N_EXPERTS - 1).astype(I32)
    n_used = (pend[-1:] // ROW_BLOCK).astype(I32)
    return counts, pstart.astype(I32), block_e, n_used


def _perm_rows(a):
    return a.reshape((N_GROUPS, GROUP_SIZE) + a.shape[1:]).swapaxes(0, 1).reshape(a.shape)


def kernel(x, c, norm1_g, norm2_g, ada_w, ada_b, w_in, sgu_norm_g, sgu_w, sgu_b, conv_w, conv_b, conv_ln_g,
           conv_ln_b, mix_norm_g, w_out, router_w, router_bias, expert_w1, expert_w3, expert_w2, shared_w1,
           shared_w3, shared_w2, final_norm_g):
    batch, seq, d = x.shape
    depth = ada_w.shape[0]
    t = batch * seq
    assert d == D_MODEL and seq % (DILATIONS[-1] * ATTN_BLOCK) == 0 and t % 512 == 0
    n_blocks = (t * TOP_K + N_EXPERTS * (ROW_BLOCK - 1) + ROW_BLOCK - 1) // ROW_BLOCK
    n_rows = n_blocks * ROW_BLOCK

    slopes = jnp.exp2(-ALIBI_MAX * jnp.arange(1, ATTN_HEADS + 1, dtype=F32) / ATTN_HEADS)
    c_pad = jnp.zeros((16, d), F32).at[:batch].set(c)
    mod = _ada_mod(c_pad, ada_w, ada_b)[:, :batch].reshape(depth, batch, ADA_CHUNKS, d)

    xc = x.reshape(t, d)
    nab = SGU_WIDTH + CONV_WIDTH
    for l in range(depth):
        za, zb, q, k, v = _inproj(xc, mod[l], norm1_g[l], w_in[l].astype(BF16), seq)
        wcat = jnp.concatenate([sgu_w[l, h] for h in range(SGU_HEADS)], axis=1)
        bmat = jnp.repeat(sgu_b[l].T, HEAD_DIM, axis=1)
        yab = _sguconv(za, zb, sgu_norm_g[l], wcat, bmat, conv_w[l], conv_b[l], conv_ln_g[l], conv_ln_b[l],
                       mix_norm_g[l, :SGU_WIDTH], mix_norm_g[l, SGU_WIDTH:nab], batch, seq)
        branches = [_attn_branch(slopes, q, k, v, batch, seq, r) for r in DILATIONS]
        rwt = _perm_rows(router_w[l].T)
        rwh = rwt.astype(BF16)
        rwl = (rwt - rwh.astype(F32)).astype(BF16)
        rb = _perm_rows(router_bias[l]).reshape(N_EXPERTS, 1)
        x2, h2rows, idx, gate, rank, cnt = _post(
            yab, [b[0] for b in branches], [b[1] for b in branches], xc, mod[l], mix_norm_g[l, nab:],
            w_out[l].astype(BF16), norm2_g[l], rwh, rwl, rb,
            shared_w1[l].astype(BF16), shared_w3[l].astype(BF16), shared_w2[l].astype(BF16), seq)
        counts, pstart, block_e, n_used = _row_layout(cnt[:, 0], n_blocks)
        xd = _dispatch(pstart, counts, idx, rank, h2rows, n_rows)
        yd = _experts(block_e, n_used, xd, expert_w1[l], expert_w3[l], expert_w2[l], n_blocks)
        xc = _combine(pstart, idx, rank, gate.T, x2, mod[l], final_norm_g, yd, seq, final=(l == depth - 1))
    return xc.reshape(batch, seq, d)
```

```python
import functools

import jax
import jax.numpy as jnp
from jax import lax
from jax.experimental import pallas as pl
from jax.experimental.pallas import tpu as pltpu

F32 = jnp.float32
BF16 = jnp.bfloat16
I32 = jnp.int32

D_MODEL = 1024
HEAD_DIM = 64
SGU_WIDTH = 256
CONV_WIDTH = 256
ATTN_WIDTH = 512
ATTN_HEADS = 8
SGU_HEADS = 4
SGU_CHUNK = 128
CONV_KERNEL = 31
CONV_HALO = 32
DILATIONS = (1, 4, 16)
ATTN_BLOCK = 128
ATTN_UNROLL = 4
ALIBI_MAX = 8.0
N_EXPERTS = 256
N_GROUPS = 8
GROUP_SIZE = N_EXPERTS // N_GROUPS
TOP_K = 8
TOP_GROUPS = 4
EXPERT_FF = 256
ROUTE_SCALE = 2.5
ROW_BLOCK = 128
ADA_CHUNKS = 6
EPS = 1e-6
LANES = 128
SUBLANES = 8
ROW_TILES = D_MODEL // LANES
NEG_BIG = -1e30
VMEM_LIMIT = 56 * 1024 * 1024


def _sigmoid(x):
    return 1.0 / (1.0 + jnp.exp(-x))


def _silu(x):
    return x * _sigmoid(x)


def _params(semantics):
    return pltpu.CompilerParams(dimension_semantics=semantics, vmem_limit_bytes=VMEM_LIMIT)


def _ada_body(c_ref, w_ref, b_ref, o_ref):
    c = c_ref[...]
    cond = _silu(c).astype(BF16)
    o_ref[...] = jnp.dot(cond, w_ref[...].astype(BF16), preferred_element_type=F32) + b_ref[...]


def _ada_mod(c_pad, ada_w, ada_b, tn=1536):
    depth, d, n = ada_w.shape
    rows = c_pad.shape[0]
    return pl.pallas_call(
        _ada_body,
        grid=(depth, n // tn),
        in_specs=[
            pl.BlockSpec((rows, d), lambda l, j: (0, 0)),
            pl.BlockSpec((None, d, tn), lambda l, j: (l, 0, j)),
            pl.BlockSpec((None, 1, tn), lambda l, j: (l, 0, j)),
        ],
        out_specs=pl.BlockSpec((None, rows, tn), lambda l, j: (l, 0, j)),
        out_shape=jax.ShapeDtypeStruct((depth, rows, n), F32),
        compiler_params=_params(("parallel", "parallel")),
        name="ada_mod",
    )(c_pad, ada_w, ada_b.reshape(depth, 1, n))


def _inproj_body(x_ref, mod_ref, g_ref, w_ref, za_ref, zb_ref, q_ref, k_ref, v_ref):
    x = x_ref[...]
    ms = jnp.mean(x * x, axis=-1, keepdims=True)
    h = x * lax.rsqrt(ms + EPS) * g_ref[...]
    h = h * (1.0 + mod_ref[1:2, :]) + mod_ref[0:1, :]
    hb = h.astype(BF16)
    outs = (za_ref, zb_ref, q_ref, k_ref, v_ref)
    for j, o_ref in enumerate(outs):
        r = jnp.dot(hb, w_ref[:, j * 512:(j + 1) * 512], preferred_element_type=F32)
        if j == 2:
            r = r * (HEAD_DIM ** -0.5)
        o_ref[...] = r.astype(o_ref.dtype)


def _inproj(x2d, mod_l, g, w_in_b, seq, tm=512):
    t, d = x2d.shape
    zout = jax.ShapeDtypeStruct((t, 512), BF16)
    aout = jax.ShapeDtypeStruct((t, 512), F32)
    ospec = pl.BlockSpec((tm, 512), lambda i: (i, 0))
    return pl.pallas_call(
        _inproj_body,
        grid=(t // tm,),
        in_specs=[
            pl.BlockSpec((tm, d), lambda i: (i, 0)),
            pl.BlockSpec((None, ADA_CHUNKS, d), lambda i: ((i * tm) // seq, 0, 0)),
            pl.BlockSpec((1, d), lambda i: (0, 0)),
            pl.BlockSpec(w_in_b.shape, lambda i: (0, 0)),
        ],
        out_specs=[ospec] * 5,
        out_shape=[zout, zout, aout, aout, aout],
        compiler_params=_params(("parallel",)),
        name="inproj",
    )(x2d, mod_l, g.reshape(1, d), w_in_b)


def _sguconv_body(za_ref, zb_ref, zbh_ref, sgug_ref, wcat_ref, bmat_ref, cw_ref, cb_ref,
                  lng_ref, lnb_ref, ga_ref, gb_ref, o_ref, ybuf, *, ts):
    i = pl.program_id(1)
    z = za_ref[...].astype(F32)
    z = 0.5 * z * (1.0 + lax.erf(z * (2.0 ** -0.5)))
    u = z[:, :SGU_WIDTH]
    v = z[:, SGU_WIDTH:]
    v = v * lax.rsqrt(jnp.mean(v * v, axis=-1, keepdims=True) + EPS) * sgug_ref[...]
    vb = v.astype(BF16)
    wt = lax.broadcasted_iota(I32, (SGU_CHUNK, SGU_HEADS * SGU_CHUNK), 0)
    ws = lax.broadcasted_iota(I32, (SGU_CHUNK, SGU_HEADS * SGU_CHUNK), 1) % SGU_CHUNK
    wb = jnp.where(ws <= wt, wcat_ref[...], 0.0).astype(BF16)
    head_of_lane = lax.broadcasted_iota(I32, (1, SGU_WIDTH), 1) // HEAD_DIM
    for c in range(ts // SGU_CHUNK):
        rows = slice(c * SGU_CHUNK, (c + 1) * SGU_CHUNK)
        vc = vb[rows, :]
        rhs = jnp.concatenate(
            [jnp.where(head_of_lane == h, vc, jnp.zeros_like(vc)) for h in range(SGU_HEADS)], axis=0)
        mixed = jnp.dot(wb, rhs, preferred_element_type=F32) + bmat_ref[...]
        ya = u[rows, :] * mixed
        ya = ya * lax.rsqrt(jnp.mean(ya * ya, axis=-1, keepdims=True) + EPS) * ga_ref[...]
        o_ref[rows, 0:SGU_WIDTH] = ya.astype(o_ref.dtype)
    zh = zbh_ref[...].astype(F32)
    yh = zh[:, :CONV_WIDTH] * _sigmoid(zh[:, CONV_WIDTH:])
    ybuf[0:CONV_HALO, :] = jnp.where(i == 0, 0.0, yh)
    zc = zb_ref[...].astype(F32)
    ybuf[CONV_HALO:CONV_HALO + ts, :] = zc[:, :CONV_WIDTH] * _sigmoid(zc[:, CONV_WIDTH:])
    rb = 64
    first_tap = CONV_HALO - (CONV_KERNEL - 1)
    for blk in range(ts // rb):
        acc = jnp.zeros((rb, CONV_WIDTH), F32) + cb_ref[...]
        for kk in range(CONV_KERNEL):
            start = blk * rb + first_tap + kk
            acc = acc + cw_ref[kk:kk + 1, :] * ybuf[start:start + rb, :]
        mu = jnp.mean(acc, axis=-1, keepdims=True)
        cen = acc - mu
        var = jnp.mean(cen * cen, axis=-1, keepdims=True)
        y = cen * lax.rsqrt(var + EPS) * lng_ref[...] + lnb_ref[...]
        y = _silu(y)
        y = y * lax.rsqrt(jnp.mean(y * y, axis=-1, keepdims=True) + EPS) * gb_ref[...]
        o_ref[blk * rb:(blk + 1) * rb, SGU_WIDTH:SGU_WIDTH + CONV_WIDTH] = y.astype(o_ref.dtype)


def _sguconv(za, zb, sgu_g, wcat, bmat, conv_w, conv_b, ln_g, ln_b, g_a, g_b, batch, seq, ts=512):
    t = za.shape[0]
    nt = seq // ts
    hpt = ts // CONV_HALO
    row = lambda a: a.reshape(1, -1)
    full = lambda a: pl.BlockSpec(a.shape, lambda b, i: (0,) * a.ndim)
    args = (row(sgu_g), wcat, bmat, conv_w, row(conv_b), row(ln_g), row(ln_b), row(g_a), row(g_b))
    return pl.pallas_call(
        functools.partial(_sguconv_body, ts=ts),
        grid=(batch, nt),
        in_specs=[
            pl.BlockSpec((ts, 512), lambda b, i: (b * nt + i, 0)),
            pl.BlockSpec((ts, 512), lambda b, i: (b * nt + i, 0)),
            pl.BlockSpec((CONV_HALO, 512), lambda b, i: (jnp.maximum((b * nt + i) * hpt - 1, 0), 0)),
        ] + [full(a) for a in args],
        out_specs=pl.BlockSpec((ts, 512), lambda b, i: (b * nt + i, 0)),
        out_shape=jax.ShapeDtypeStruct((t, 512), BF16),
        scratch_shapes=[pltpu.VMEM((CONV_HALO + ts, CONV_WIDTH), F32)],
        compiler_params=_params(("parallel", "arbitrary")),
        name="sguconv",
    )(za, zb, zb, *args)


def _attn_body(slopes_ref, q_ref, k_ref, v_ref, y_ref, bias_ref, o1, l1, o2, l2, o3, l3, *, seq):
    hp = pl.program_id(1)
    nj = seq // ATTN_BLOCK
    qi = lax.broadcasted_iota(I32, (ATTN_BLOCK, ATTN_BLOCK), 0)
    kj = lax.broadcasted_iota(I32, (ATTN_BLOCK, ATTN_BLOCK), 1)
    d_cur = qi - kj
    d_prev = d_cur + ATTN_BLOCK
    lane = lax.broadcasted_iota(I32, (1, LANES), 1)
    low = lane < HEAD_DIM
    dn = (((1,), (1,)), ((), ()))
    scratch = ((o1, l1), (o2, l2), (o3, l3))

    for (o_scr, l_scr), r in zip(scratch, DILATIONS):
        nb = nj // r
        for hh in range(2):
            slope = slopes_ref[hp * 2 + hh]
            bias_ref[2 * hh] = jnp.where(d_cur >= 0, -slope * (d_cur * r).astype(F32), NEG_BIG)
            bias_ref[2 * hh + 1] = jnp.where(d_cur <= 0, -slope * (d_prev * r).astype(F32), NEG_BIG)

        def rows_at(rho, n, r=r):
            start = rho + (r * ATTN_BLOCK) * n
            if r == 1:
                return pl.ds(pl.multiple_of(start, ATTN_BLOCK), ATTN_BLOCK)
            return pl.ds(start, ATTN_BLOCK, stride=r)

        def block(j, nb=nb, rows_at=rows_at, o_scr=o_scr, l_scr=l_scr):
            rho = lax.div(j, nb)
            n = lax.rem(j, nb)
            cur = rows_at(rho, n)
            prv = rows_at(rho, jnp.maximum(n - 1, 0))
            no_prev = jnp.where(n == 0, NEG_BIG, 0.0)
            qn = q_ref[cur, :].astype(BF16)
            kn = k_ref[cur, :].astype(BF16)
            vn = v_ref[cur, :].astype(BF16)
            kp = k_ref[prv, :].astype(BF16)
            vp = v_ref[prv, :].astype(BF16)
            outs, lses = [], []
            for hh in range(2):
                mine = low if hh == 0 else jnp.logical_not(low)
                qm = jnp.where(mine, qn, jnp.zeros_like(qn))
                s_c = lax.dot_general(qm, kn, dn, preferred_element_type=F32) + bias_ref[2 * hh]
                s_p = lax.dot_general(qm, kp, dn, preferred_element_type=F32) + (bias_ref[2 * hh + 1] + no_prev)
                m = jnp.maximum(jnp.max(s_c, axis=-1, keepdims=True), jnp.max(s_p, axis=-1, keepdims=True))
                p_c = jnp.exp(s_c - m)
                p_p = jnp.exp(s_p - m)
                den = jnp.sum(p_c, axis=-1, keepdims=True) + jnp.sum(p_p, axis=-1, keepdims=True)
                acc = (jnp.dot(p_c.astype(BF16), vn, preferred_element_type=F32)
                       + jnp.dot(p_p.astype(BF16), vp, preferred_element_type=F32))
                outs.append(acc / den)
                lses.append(m + jnp.log(den))
            o_scr[cur, :] = jnp.where(low, outs[0], outs[1])
            l_scr[cur, :] = jnp.where(low, lses[0], lses[1])

        def step(it, carry, block=block):
            for u in range(ATTN_UNROLL):
                block(it * ATTN_UNROLL + u)
            return carry

        lax.fori_loop(0, nj // ATTN_UNROLL, step, 0)

    def merge(n, carry):
        rows = pl.ds(pl.multiple_of(n * ATTN_BLOCK, ATTN_BLOCK), ATTN_BLOCK)
        la, lb, lc = l1[rows, :], l2[rows, :], l3[rows, :]
        mx = jnp.maximum(jnp.maximum(la, lb), lc)
        ea, eb, ec = jnp.exp(la - mx), jnp.exp(lb - mx), jnp.exp(lc - mx)
        y = (ea * o1[rows, :] + eb * o2[rows, :] + ec * o3[rows, :]) / (ea + eb + ec)
        y_ref[rows, :] = y.astype(y_ref.dtype)
        return carry

    lax.fori_loop(0, nj, merge, 0)


def _attention(slopes, q, k, v, batch, seq):
    npairs = ATTN_WIDTH // LANES
    spec = pl.BlockSpec((seq, LANES), lambda b, hp, *_: (b, hp))
    seq_buf = pltpu.VMEM((seq, LANES), F32)
    return pl.pallas_call(
        functools.partial(_attn_body, seq=seq),
        grid_spec=pltpu.PrefetchScalarGridSpec(
            num_scalar_prefetch=1,
            grid=(batch, npairs),
            in_specs=[spec, spec, spec],
            out_specs=spec,
            scratch_shapes=[pltpu.VMEM((4, ATTN_BLOCK, ATTN_BLOCK), F32)] + [seq_buf] * 6,
        ),
        out_shape=jax.ShapeDtypeStruct((batch * seq, ATTN_WIDTH), BF16),
        compiler_params=_params(("parallel", "parallel")),
        name="attention",
    )(slopes, q, k, v)


def _route_chunk(logits, rbias, run, upper, ones):
    nj = GROUP_SIZE
    gidx = lax.broadcasted_iota(I32, (N_GROUPS, LANES), 0)
    sc = [_sigmoid(logits[j * 8:(j + 1) * 8, :]) for j in range(nj)]
    bs = [sc[j] + rbias[j * 8:(j + 1) * 8, :] for j in range(nj)]
    tree = lambda f, xs: functools.reduce(f, xs)
    neg_inf = -jnp.inf
    m1 = tree(jnp.maximum, bs)
    first = tree(jnp.minimum, [jnp.where(bs[j] == m1, j, nj) for j in range(nj)])
    m2 = tree(jnp.maximum, [jnp.where(first == j, neg_inf, bs[j]) for j in range(nj)])
    gs = m1 + m2
    beaten = jnp.zeros((N_GROUPS, LANES), I32)
    for g2 in range(N_GROUPS):
        other = gs[g2:g2 + 1, :]
        beats = (other > gs) | ((other == gs) & (g2 < gidx))
        beaten = beaten + beats.astype(I32)
    gsel = beaten < TOP_GROUPS
    mk = [jnp.where(gsel, bs[j], neg_inf) for j in range(nj)]
    eid = [gidx * GROUP_SIZE + j for j in range(nj)]
    selm = [jnp.zeros((N_GROUPS, LANES), F32) for _ in range(nj)]
    idx_rows, gate_rows = [], []
    for _ in range(TOP_K):
        mx = jnp.max(tree(jnp.maximum, mk), axis=0, keepdims=True)
        cand = tree(jnp.minimum, [jnp.where(mk[j] == mx, eid[j], N_EXPERTS) for j in range(nj)])
        sel = jnp.min(cand, axis=0, keepdims=True)
        hit = [eid[j] == sel for j in range(nj)]
        gk = jnp.sum(tree(jnp.add, [jnp.where(hit[j], sc[j], 0.0) for j in range(nj)]), axis=0, keepdims=True)
        mk = [jnp.where(hit[j], neg_inf, mk[j]) for j in range(nj)]
        selm = [selm[j] + hit[j].astype(F32) for j in range(nj)]
        idx_rows.append(sel)
        gate_rows.append(gk)
    gsum = tree(jnp.add, gate_rows)
    gate = jnp.concatenate([g / gsum * ROUTE_SCALE for g in gate_rows], axis=0)
    idx = jnp.concatenate(idx_rows, axis=0)
    msel = jnp.concatenate(selm, axis=0).astype(BF16)
    before = run + jnp.dot(msel, upper, preferred_element_type=F32)
    rank_rows = []
    for kk in range(TOP_K):
        contrib = [jnp.where(eid[j] == idx_rows[kk], before[j * 8:(j + 1) * 8, :], 0.0) for j in range(nj)]
        rank_rows.append(jnp.sum(tree(jnp.add, contrib), axis=0, keepdims=True))
    rank = jnp.concatenate(rank_rows, axis=0).astype(I32)
    run = run + jnp.dot(msel, ones, preferred_element_type=F32)
    return idx, gate, rank, run


def _post_body(yab_ref, yc_ref, x_ref, mod_ref, gc_ref, wout_ref,
               n2g_ref, rwh_ref, rwl_ref, rb_ref, sw1_ref, sw3_ref, sw2_ref,
               x2_ref, h2_ref, idx_ref, gate_ref, rank_ref, cnt_ref, run_ref, *, tm):
    i = pl.program_id(0)

    @pl.when(i == 0)
    def _():
        run_ref[...] = jnp.zeros_like(run_ref)

    yc = yc_ref[...].astype(F32)
    yc = yc * lax.rsqrt(jnp.mean(yc * yc, axis=-1, keepdims=True) + EPS) * gc_ref[...]
    nab = SGU_WIDTH + CONV_WIDTH
    mix = (jnp.dot(yab_ref[...], wout_ref[0:nab, :], preferred_element_type=F32)
           + jnp.dot(yc.astype(BF16), wout_ref[nab:, :], preferred_element_type=F32))
    x1 = x_ref[...] + mod_ref[2:3, :] * mix
    h2 = x1 * lax.rsqrt(jnp.mean(x1 * x1, axis=-1, keepdims=True) + EPS) * n2g_ref[...]
    h2 = h2 * (1.0 + mod_ref[4:5, :]) + mod_ref[3:4, :]
    for c in range(ROW_TILES):
        h2_ref[pl.ds(c, tm, stride=ROW_TILES), :] = h2[:, c * LANES:(c + 1) * LANES]
    hb = h2.astype(BF16)
    a = jnp.dot(hb, sw1_ref[...], preferred_element_type=F32)
    b = jnp.dot(hb, sw3_ref[...], preferred_element_type=F32)
    sh = jnp.dot((_silu(a) * b).astype(BF16), sw2_ref[...], preferred_element_type=F32)
    x2_ref[...] = x1 + mod_ref[5:6, :] * sh
    hl = (h2 - hb.astype(F32)).astype(BF16)
    dn = (((1,), (1,)), ((), ()))
    logits = (lax.dot_general(rwh_ref[...], hb, dn, preferred_element_type=F32)
              + lax.dot_general(rwl_ref[...], hb, dn, preferred_element_type=F32)
              + lax.dot_general(rwh_ref[...], hl, dn, preferred_element_type=F32))
    tt = lax.broadcasted_iota(I32, (LANES, LANES), 0)
    tc = lax.broadcasted_iota(I32, (LANES, LANES), 1)
    upper = (tt < tc).astype(BF16)
    ones = jnp.ones((LANES, LANES), BF16)
    run = run_ref[...]
    for c in range(tm // LANES):
        cols = slice(c * LANES, (c + 1) * LANES)
        idx, gate, rank, run = _route_chunk(logits[:, cols], rb_ref[...], run, upper, ones)
        idx_ref[:, cols] = idx
        gate_ref[:, cols] = gate
        rank_ref[:, cols] = rank
    run_ref[...] = run
    cnt_ref[...] = run


def _post(yab, yc, x2d, mod_l, g_c, wout_b, n2g, rwh, rwl, rb, sw1, sw3, sw2, seq, tm=256):
    t, d = x2d.shape
    row = lambda a: a.reshape(1, -1)
    tok = lambda w: pl.BlockSpec((tm, w), lambda i: (i, 0))
    full = lambda a: pl.BlockSpec(a.shape, lambda i: (0,) * a.ndim)
    consts = (row(g_c), wout_b, row(n2g), rwh, rwl, rb, sw1, sw3, sw2)
    kt = lambda dt: jax.ShapeDtypeStruct((TOP_K, t), dt)
    kspec = pl.BlockSpec((TOP_K, tm), lambda i: (0, i))
    return pl.pallas_call(
        functools.partial(_post_body, tm=tm),
        grid=(t // tm,),
        in_specs=[tok(512), tok(512), tok(d), pl.BlockSpec((None, ADA_CHUNKS, d), lambda i: ((i * tm) // seq, 0, 0))]
        + [full(a) for a in consts],
        out_specs=[tok(d), pl.BlockSpec((tm * ROW_TILES, LANES), lambda i: (i, 0)), kspec, kspec, kspec,
                   pl.BlockSpec((N_EXPERTS, LANES), lambda i: (0, 0))],
        out_shape=[jax.ShapeDtypeStruct((t, d), F32), jax.ShapeDtypeStruct((t * ROW_TILES, LANES), F32),
                   kt(I32), kt(F32), kt(I32), jax.ShapeDtypeStruct((N_EXPERTS, LANES), F32)],
        scratch_shapes=[pltpu.VMEM((N_EXPERTS, LANES), F32)],
        compiler_params=_params(("arbitrary",)),
        name="post",
    )(yab, yc, x2d, mod_l, *consts)


_PAD_BITS = (64, 32, 16, 8, 4, 2, 1)
BLOCK_SLABS = ROW_BLOCK * ROW_TILES


def _row_slab(ref, row):
    return ref.at[pl.ds(pl.multiple_of(row * ROW_TILES, ROW_TILES), ROW_TILES), :]


def _dispatch_body(pstart_ref, cnt_ref, nu_ref, idx_ref, rank_ref, h_ref, xd_ref, zbuf, sem, zsem, *, tq, n_blocks):
    i = pl.program_id(0)

    @pl.when(i == 0)
    def _():
        zbuf[...] = jnp.zeros_like(zbuf)

        def pad_copies(e, act):
            base = pstart_ref[e] + cnt_ref[e]
            npad = (-cnt_ref[e]) & (ROW_BLOCK - 1)
            for bit in _PAD_BITS:
                nrows = bit * ROW_TILES

                @pl.when((npad & bit) != 0)
                def _(base=base, nrows=nrows):
                    dst = xd_ref.at[pl.ds(pl.multiple_of(base * ROW_TILES, ROW_TILES), nrows), :]
                    act(pltpu.make_async_copy(zbuf.at[pl.ds(0, nrows), :], dst, zsem))

                base = base + (npad & bit)

        def tail_copy(blk, act):
            dst = xd_ref.at[pl.ds(pl.multiple_of(blk * BLOCK_SLABS, BLOCK_SLABS), BLOCK_SLABS), :]
            act(pltpu.make_async_copy(zbuf, dst, zsem))

        def run_all(act):
            def per_expert(e, carry):
                pad_copies(e, act)
                return carry

            def per_tail(blk, carry):
                tail_copy(blk, act)
                return carry

            lax.fori_loop(0, N_EXPERTS, per_expert, 0)
            lax.fori_loop(nu_ref[0], n_blocks, per_tail, 0)

        run_all(lambda cp: cp.start())
        run_all(lambda cp: cp.wait())

    def token(t, carry):
        src = _row_slab(h_ref, t)
        for kk in range(TOP_K):
            dest = pstart_ref[idx_ref[kk, t]] + rank_ref[kk, t]
            pltpu.make_async_copy(src, _row_slab(xd_ref, dest), sem).start(priority=kk % 2)
        return carry

    lax.fori_loop(0, tq, token, 0)
    for kk in range(TOP_K):
        pltpu.make_async_copy(h_ref, xd_ref.at[pl.ds(0, tq * ROW_TILES), :], sem).wait()


def _dispatch(pstart, counts, n_used, idx, rank, h2rows, n_blocks, tq=512):
    t = idx.shape[1]
    smem = pl.BlockSpec((TOP_K, tq), lambda i, *_: (0, i), memory_space=pltpu.SMEM)
    return pl.pallas_call(
        functools.partial(_dispatch_body, tq=tq, n_blocks=n_blocks),
        grid_spec=pltpu.PrefetchScalarGridSpec(
            num_scalar_prefetch=3,
            grid=(t // tq,),
            in_specs=[smem, smem, pl.BlockSpec((tq * ROW_TILES, LANES), lambda i, *_: (i, 0))],
            out_specs=pl.BlockSpec(memory_space=pl.ANY),
            scratch_shapes=[pltpu.VMEM((BLOCK_SLABS, LANES), F32),
                            pltpu.SemaphoreType.DMA, pltpu.SemaphoreType.DMA],
        ),
        out_shape=jax.ShapeDtypeStruct((n_blocks * BLOCK_SLABS, LANES), F32),
        compiler_params=_params(("arbitrary",)),
        name="dispatch",
    )(pstart, counts, n_used, idx, rank, h2rows)


def _expert_body(be_ref, nu_ref, xd_ref, w1_ref, w3_ref, w2_ref, yd_ref, w1b, w3b, w2b):
    i = pl.program_id(0)
    changed = (i == 0) | (be_ref[i] != be_ref[jnp.maximum(i - 1, 0)])

    @pl.when(changed)
    def _():
        w1b[...] = w1_ref[...].astype(BF16)
        w3b[...] = w3_ref[...].astype(BF16)
        w2b[...] = w2_ref[...].astype(BF16)

    @pl.when(i < nu_ref[0])
    def _():
        x = jnp.concatenate([xd_ref[pl.ds(c, ROW_BLOCK, stride=ROW_TILES), :] for c in range(ROW_TILES)], axis=1)
        xb = x.astype(BF16)
        a = jnp.dot(xb, w1b[...], preferred_element_type=F32)
        b = jnp.dot(xb, w3b[...], preferred_element_type=F32)
        y = jnp.dot((_silu(a) * b).astype(BF16), w2b[...], preferred_element_type=F32)
        for c in range(ROW_TILES):
            yd_ref[pl.ds(c, ROW_BLOCK, stride=ROW_TILES), :] = y[:, c * LANES:(c + 1) * LANES]

    @pl.when(i >= nu_ref[0])
    def _():
        yd_ref[...] = jnp.zeros_like(yd_ref)


def _experts(block_e, n_used, xd, w1, w3, w2, layer, n_blocks):
    d, ff = w1.shape[2], w1.shape[3]
    wspec = lambda a, b: pl.BlockSpec((None, None, a, b), lambda i, be, nu: (layer, be[i], 0, 0))
    return pl.pallas_call(
        _expert_body,
        grid_spec=pltpu.PrefetchScalarGridSpec(
            num_scalar_prefetch=2,
            grid=(n_blocks,),
            in_specs=[pl.BlockSpec((BLOCK_SLABS, LANES), lambda i, be, nu: (jnp.minimum(i, nu[0] - 1), 0)),
                      wspec(d, ff), wspec(d, ff), wspec(ff, d)],
            out_specs=pl.BlockSpec((BLOCK_SLABS, LANES), lambda i, be, nu: (i, 0)),
            scratch_shapes=[pltpu.VMEM((d, ff), BF16), pltpu.VMEM((d, ff), BF16), pltpu.VMEM((ff, d), BF16)],
        ),
        out_shape=jax.ShapeDtypeStruct(xd.shape, F32),
        compiler_params=_params(("arbitrary",)),
        name="experts",
    )(block_e, n_used, xd, w1, w3, w2)


def _combine_body(pstart_ref, idx_ref, rank_ref, gt_ref, x2_ref, mod_ref, fg_ref, yd_ref, o_ref, buf, sem,
                  *, tq, final):
    tile_rows = tq * ROW_TILES

    def token(t, carry):
        for kk in range(TOP_K):
            dest = pstart_ref[idx_ref[kk, t]] + rank_ref[kk, t]
            dst = buf.at[pl.ds(pl.multiple_of(kk * tile_rows + t * ROW_TILES, ROW_TILES), ROW_TILES), :]
            pltpu.make_async_copy(_row_slab(yd_ref, dest), dst, sem).start(priority=kk % 2)
        return carry

    lax.fori_loop(0, tq, token, 0)
    for kk in range(TOP_K):
        pltpu.make_async_copy(yd_ref.at[pl.ds(0, tile_rows), :],
                              buf.at[pl.ds(kk * tile_rows, tile_rows), :], sem).wait()
    gates = gt_ref[...]
    outs = []
    for c in range(ROW_TILES):
        routed = jnp.zeros((tq, LANES), F32)
        for kk in range(TOP_K):
            routed = routed + gates[:, kk:kk + 1] * buf[pl.ds(kk * tile_rows + c, tq, stride=ROW_TILES), :]
        cols = slice(c * LANES, (c + 1) * LANES)
        outs.append(x2_ref[:, cols] + mod_ref[5:6, cols] * routed)
    if final:
        ms = functools.reduce(jnp.add, [jnp.sum(o * o, axis=-1, keepdims=True) for o in outs]) / D_MODEL
        inv = lax.rsqrt(ms + EPS)
        outs = [o * inv * fg_ref[:, c * LANES:(c + 1) * LANES] for c, o in enumerate(outs)]
    for c, o in enumerate(outs):
        o_ref[:, c * LANES:(c + 1) * LANES] = o


def _combine(pstart, idx, rank, gate_t, x2, mod_l, final_g, yd, seq, final, tq=256):
    t, d = x2.shape
    smem = pl.BlockSpec((TOP_K, tq), lambda i, *_: (0, i), memory_space=pltpu.SMEM)
    return pl.pallas_call(
        functools.partial(_combine_body, tq=tq, final=final),
        grid_spec=pltpu.PrefetchScalarGridSpec(
            num_scalar_prefetch=1,
            grid=(t // tq,),
            in_specs=[smem, smem,
                      pl.BlockSpec((tq, TOP_K), lambda i, *_: (i, 0)),
                      pl.BlockSpec((tq, d), lambda i, *_: (i, 0)),
                      pl.BlockSpec((None, ADA_CHUNKS, d), lambda i, *_: ((i * tq) // seq, 0, 0)),
                      pl.BlockSpec((1, d), lambda i, *_: (0, 0)),
                      pl.BlockSpec(memory_space=pl.ANY)],
            out_specs=pl.BlockSpec((tq, d), lambda i, *_: (i, 0)),
            scratch_shapes=[pltpu.VMEM((TOP_K * tq * ROW_TILES, LANES), F32), pltpu.SemaphoreType.DMA],
        ),
        out_shape=jax.ShapeDtypeStruct((t, d), F32),
        compiler_params=_params(("arbitrary",)),
        name="combine",
    )(pstart, idx, rank, gate_t, x2, mod_l, final_g.reshape(1, d), yd)


def _row_layout(counts_perm, n_blocks):
    counts = counts_perm.reshape(GROUP_SIZE, N_GROUPS).T.reshape(N_EXPERTS).astype(I32)
    padded = (counts + ROW_BLOCK - 1) // ROW_BLOCK * ROW_BLOCK
    pend = jnp.cumsum(padded)
    pstart = pend - padded
    block_row0 = jnp.arange(n_blocks, dtype=I32) * ROW_BLOCK
    block_e = jnp.minimum(jnp.sum((pend[None, :] <= block_row0[:, None]).astype(I32), axis=1), N_EXPERTS - 1)
    n_used = (pend[-1:] // ROW_BLOCK).astype(I32)
    return counts, pstart.astype(I32), block_e.astype(I32), n_used


def _perm_rows(a):
    return a.reshape((N_GROUPS, GROUP_SIZE) + a.shape[1:]).swapaxes(0, 1).reshape(a.shape)


def kernel(x, c, norm1_g, norm2_g, ada_w, ada_b, w_in, sgu_norm_g, sgu_w, sgu_b, conv_w, conv_b, conv_ln_g,
           conv_ln_b, mix_norm_g, w_out, router_w, router_bias, expert_w1, expert_w3, expert_w2, shared_w1,
           shared_w3, shared_w2, final_norm_g):
    batch, seq, d = x.shape
    depth = ada_w.shape[0]
    t = batch * seq
    assert d == D_MODEL and seq % (DILATIONS[-1] * ATTN_BLOCK) == 0 and t % 512 == 0
    assert (seq // ATTN_BLOCK) % ATTN_UNROLL == 0
    n_blocks = (t * TOP_K + N_EXPERTS * (ROW_BLOCK - 1) + ROW_BLOCK - 1) // ROW_BLOCK

    slopes = jnp.exp2(-ALIBI_MAX * jnp.arange(1, ATTN_HEADS + 1, dtype=F32) / ATTN_HEADS)
    c_pad = jnp.zeros((16, d), F32).at[:batch].set(c)
    mod = _ada_mod(c_pad, ada_w, ada_b)[:, :batch].reshape(depth, batch, ADA_CHUNKS, d)

    xc = x.reshape(t, d)
    nab = SGU_WIDTH + CONV_WIDTH
    for l in range(depth):
        za, zb, q, k, v = _inproj(xc, mod[l], norm1_g[l], w_in[l].astype(BF16), seq)
        wcat = jnp.concatenate([sgu_w[l, h] for h in range(SGU_HEADS)], axis=1)
        bmat = jnp.repeat(sgu_b[l].T, HEAD_DIM, axis=1)
        yab = _sguconv(za, zb, sgu_norm_g[l], wcat, bmat, conv_w[l], conv_b[l], conv_ln_g[l], conv_ln_b[l],
                       mix_norm_g[l, :SGU_WIDTH], mix_norm_g[l, SGU_WIDTH:nab], batch, seq)
        yc = _attention(slopes, q, k, v, batch, seq)
        rwt = _perm_rows(router_w[l].T)
        rwh = rwt.astype(BF16)
        rwl = (rwt - rwh.astype(F32)).astype(BF16)
        rb = _perm_rows(router_bias[l]).reshape(N_EXPERTS, 1)
        x2, h2rows, idx, gate, rank, cnt = _post(
            yab, yc, xc, mod[l], mix_norm_g[l, nab:], w_out[l].astype(BF16), norm2_g[l], rwh, rwl, rb,
            shared_w1[l].astype(BF16), shared_w3[l].astype(BF16), shared_w2[l].astype(BF16), seq)
        counts, pstart, block_e, n_used = _row_layout(cnt[:, 0], n_blocks)
        xd = _dispatch(pstart, counts, n_used, idx, rank, h2rows, n_blocks)
        yd = _experts(block_e, n_used, xd, expert_w1, expert_w3, expert_w2, l, n_blocks)
        xc = _combine(pstart, idx, rank, gate.T, x2, mod[l], final_norm_g, yd, seq, final=(l == depth - 1))
    return xc.reshape(batch, seq, d)
```

```python
import functools

import jax
import jax.numpy as jnp
from jax import lax
from jax.experimental import pallas as pl
from jax.experimental.pallas import tpu as pltpu

F32 = jnp.float32
BF16 = jnp.bfloat16
I32 = jnp.int32

D_MODEL = 1024
HEAD_DIM = 64
SGU_WIDTH = 256
CONV_WIDTH = 256
ATTN_WIDTH = 512
ATTN_HEADS = 8
SGU_HEADS = 4
SGU_CHUNK = 128
CONV_KERNEL = 31
CONV_HALO = 32
DILATIONS = (1, 4, 16)
ATTN_BLOCK = 128
ATTN_UNROLL = 4
ALIBI_MAX = 8.0
N_EXPERTS = 256
N_GROUPS = 8
GROUP_SIZE = N_EXPERTS // N_GROUPS
TOP_K = 8
TOP_GROUPS = 4
EXPERT_FF = 256
ROUTE_SCALE = 2.5
ROW_BLOCK = 128
ADA_CHUNKS = 6
EPS = 1e-6
LANES = 128
SUBLANES = 8
ROW_TILES = D_MODEL // LANES
NEG_BIG = -1e30
VMEM_LIMIT = 56 * 1024 * 1024


def _sigmoid(x):
    return 1.0 / (1.0 + jnp.exp(-x))


def _silu(x):
    return x * _sigmoid(x)


def _params(semantics):
    return pltpu.CompilerParams(dimension_semantics=semantics, vmem_limit_bytes=VMEM_LIMIT)


def _ada_body(c_ref, w_ref, b_ref, o_ref):
    c = c_ref[...]
    cond = _silu(c).astype(BF16)
    o_ref[...] = jnp.dot(cond, w_ref[...].astype(BF16), preferred_element_type=F32) + b_ref[...]


def _ada_mod(c_pad, ada_w, ada_b, tn=1536):
    depth, d, n = ada_w.shape
    rows = c_pad.shape[0]
    return pl.pallas_call(
        _ada_body,
        grid=(depth, n // tn),
        in_specs=[
            pl.BlockSpec((rows, d), lambda l, j: (0, 0)),
            pl.BlockSpec((None, d, tn), lambda l, j: (l, 0, j)),
            pl.BlockSpec((None, 1, tn), lambda l, j: (l, 0, j)),
        ],
        out_specs=pl.BlockSpec((None, rows, tn), lambda l, j: (l, 0, j)),
        out_shape=jax.ShapeDtypeStruct((depth, rows, n), F32),
        compiler_params=_params(("parallel", "parallel")),
        name="ada_mod",
    )(c_pad, ada_w, ada_b.reshape(depth, 1, n))


def _inproj_body(x_ref, mod_ref, g_ref, w_ref, za_ref, zb_ref, q_ref, k_ref, v_ref):
    x = x_ref[...]
    ms = jnp.mean(x * x, axis=-1, keepdims=True)
    h = x * lax.rsqrt(ms + EPS) * g_ref[...]
    h = h * (1.0 + mod_ref[1:2, :]) + mod_ref[0:1, :]
    hb = h.astype(BF16)
    outs = (za_ref, zb_ref, q_ref, k_ref, v_ref)
    for j, o_ref in enumerate(outs):
        r = jnp.dot(hb, w_ref[:, j * 512:(j + 1) * 512], preferred_element_type=F32)
        if j == 2:
            r = r * (HEAD_DIM ** -0.5)
        o_ref[...] = r.astype(o_ref.dtype)


def _inproj(x2d, mod_l, g, w_in_b, seq, tm=512):
    t, d = x2d.shape
    zout = jax.ShapeDtypeStruct((t, 512), BF16)
    aout = jax.ShapeDtypeStruct((t, 512), F32)
    ospec = pl.BlockSpec((tm, 512), lambda i: (i, 0))
    return pl.pallas_call(
        _inproj_body,
        grid=(t // tm,),
        in_specs=[
            pl.BlockSpec((tm, d), lambda i: (i, 0)),
            pl.BlockSpec((None, ADA_CHUNKS, d), lambda i: ((i * tm) // seq, 0, 0)),
            pl.BlockSpec((1, d), lambda i: (0, 0)),
            pl.BlockSpec(w_in_b.shape, lambda i: (0, 0)),
        ],
        out_specs=[ospec] * 5,
        out_shape=[zout, zout, aout, aout, aout],
        compiler_params=_params(("parallel",)),
        name="inproj",
    )(x2d, mod_l, g.reshape(1, d), w_in_b)


def _sguconv_body(za_ref, zb_ref, zbh_ref, sgug_ref, wcat_ref, bmat_ref, cw_ref, cb_ref,
                  lng_ref, lnb_ref, ga_ref, gb_ref, o_ref, ybuf, *, ts):
    i = pl.program_id(1)
    z = za_ref[...].astype(F32)
    z = 0.5 * z * (1.0 + lax.erf(z * (2.0 ** -0.5)))
    u = z[:, :SGU_WIDTH]
    v = z[:, SGU_WIDTH:]
    v = v * lax.rsqrt(jnp.mean(v * v, axis=-1, keepdims=True) + EPS) * sgug_ref[...]
    vb = v.astype(BF16)
    wt = lax.broadcasted_iota(I32, (SGU_CHUNK, SGU_HEADS * SGU_CHUNK), 0)
    ws = lax.broadcasted_iota(I32, (SGU_CHUNK, SGU_HEADS * SGU_CHUNK), 1) % SGU_CHUNK
    wb = jnp.where(ws <= wt, wcat_ref[...], 0.0).astype(BF16)
    head_of_lane = lax.broadcasted_iota(I32, (1, SGU_WIDTH), 1) // HEAD_DIM
    for c in range(ts // SGU_CHUNK):
        rows = slice(c * SGU_CHUNK, (c + 1) * SGU_CHUNK)
        vc = vb[rows, :]
        rhs = jnp.concatenate(
            [jnp.where(head_of_lane == h, vc, jnp.zeros_like(vc)) for h in range(SGU_HEADS)], axis=0)
        mixed = jnp.dot(wb, rhs, preferred_element_type=F32) + bmat_ref[...]
        ya = u[rows, :] * mixed
        ya = ya * lax.rsqrt(jnp.mean(ya * ya, axis=-1, keepdims=True) + EPS) * ga_ref[...]
        o_ref[rows, 0:SGU_WIDTH] = ya.astype(o_ref.dtype)
    zh = zbh_ref[...].astype(F32)
    yh = zh[:, :CONV_WIDTH] * _sigmoid(zh[:, CONV_WIDTH:])
    ybuf[0:CONV_HALO, :] = jnp.where(i == 0, 0.0, yh)
    zc = zb_ref[...].astype(F32)
    ybuf[CONV_HALO:CONV_HALO + ts, :] = zc[:, :CONV_WIDTH] * _sigmoid(zc[:, CONV_WIDTH:])
    rb = 64
    first_tap = CONV_HALO - (CONV_KERNEL - 1)
    for blk in range(ts // rb):
        acc = jnp.zeros((rb, CONV_WIDTH), F32) + cb_ref[...]
        for kk in range(CONV_KERNEL):
            start = blk * rb + first_tap + kk
            acc = acc + cw_ref[kk:kk + 1, :] * ybuf[start:start + rb, :]
        mu = jnp.mean(acc, axis=-1, keepdims=True)
        cen = acc - mu
        var = jnp.mean(cen * cen, axis=-1, keepdims=True)
        y = cen * lax.rsqrt(var + EPS) * lng_ref[...] + lnb_ref[...]
        y = _silu(y)
        y = y * lax.rsqrt(jnp.mean(y * y, axis=-1, keepdims=True) + EPS) * gb_ref[...]
        o_ref[blk * rb:(blk + 1) * rb, SGU_WIDTH:SGU_WIDTH + CONV_WIDTH] = y.astype(o_ref.dtype)


def _sguconv(za, zb, sgu_g, wcat, bmat, conv_w, conv_b, ln_g, ln_b, g_a, g_b, batch, seq, ts=512):
    t = za.shape[0]
    nt = seq // ts
    hpt = ts // CONV_HALO
    row = lambda a: a.reshape(1, -1)
    full = lambda a: pl.BlockSpec(a.shape, lambda b, i: (0,) * a.ndim)
    args = (row(sgu_g), wcat, bmat, conv_w, row(conv_b), row(ln_g), row(ln_b), row(g_a), row(g_b))
    return pl.pallas_call(
        functools.partial(_sguconv_body, ts=ts),
        grid=(batch, nt),
        in_specs=[
            pl.BlockSpec((ts, 512), lambda b, i: (b * nt + i, 0)),
            pl.BlockSpec((ts, 512), lambda b, i: (b * nt + i, 0)),
            pl.BlockSpec((CONV_HALO, 512), lambda b, i: (jnp.maximum((b * nt + i) * hpt - 1, 0), 0)),
        ] + [full(a) for a in args],
        out_specs=pl.BlockSpec((ts, 512), lambda b, i: (b * nt + i, 0)),
        out_shape=jax.ShapeDtypeStruct((t, 512), BF16),
        scratch_shapes=[pltpu.VMEM((CONV_HALO + ts, CONV_WIDTH), F32)],
        compiler_params=_params(("parallel", "arbitrary")),
        name="sguconv",
    )(za, zb, zb, *args)


def _attn_body(slopes_ref, q_ref, k_ref, v_ref, y_ref, bias_ref, o1, l1, o2, l2, o3, l3, *, seq):
    hp = pl.program_id(1)
    nj = seq // ATTN_BLOCK
    qi = lax.broadcasted_iota(I32, (ATTN_BLOCK, ATTN_BLOCK), 0)
    kj = lax.broadcasted_iota(I32, (ATTN_BLOCK, ATTN_BLOCK), 1)
    d_cur = qi - kj
    d_prev = d_cur + ATTN_BLOCK
    lane = lax.broadcasted_iota(I32, (1, LANES), 1)
    low = lane < HEAD_DIM
    dn = (((1,), (1,)), ((), ()))
    scratch = ((o1, l1), (o2, l2), (o3, l3))

    for (o_scr, l_scr), r in zip(scratch, DILATIONS):
        nb = nj // r
        for hh in range(2):
            slope = slopes_ref[hp * 2 + hh]
            bias_ref[2 * hh] = jnp.where(d_cur >= 0, -slope * (d_cur * r).astype(F32), NEG_BIG)
            bias_ref[2 * hh + 1] = jnp.where(d_cur <= 0, -slope * (d_prev * r).astype(F32), NEG_BIG)

        def rows_at(rho, n, r=r):
            start = rho + (r * ATTN_BLOCK) * n
            if r == 1:
                return pl.ds(pl.multiple_of(start, ATTN_BLOCK), ATTN_BLOCK)
            return pl.ds(start, ATTN_BLOCK, stride=r)

        def step(it, carry, nb=nb, rows_at=rows_at, o_scr=o_scr, l_scr=l_scr):
            blocks = []
            for u in range(ATTN_UNROLL):
                j = it * ATTN_UNROLL + u
                rho = lax.div(j, nb)
                n = lax.rem(j, nb)
                cur = rows_at(rho, n)
                prv = rows_at(rho, jnp.maximum(n - 1, 0))
                no_prev = jnp.where(n == 0, NEG_BIG, 0.0)
                blocks.append((cur, no_prev, q_ref[cur, :].astype(BF16), k_ref[cur, :].astype(BF16),
                               v_ref[cur, :].astype(BF16), k_ref[prv, :].astype(BF16), v_ref[prv, :].astype(BF16)))
            scores = []
            for cur, no_prev, qn, kn, vn, kp, vp in blocks:
                for hh in range(2):
                    mine = low if hh == 0 else jnp.logical_not(low)
                    qm = jnp.where(mine, qn, jnp.zeros_like(qn))
                    s_c = lax.dot_general(qm, kn, dn, preferred_element_type=F32) + bias_ref[2 * hh]
                    s_p = lax.dot_general(qm, kp, dn, preferred_element_type=F32) + (bias_ref[2 * hh + 1] + no_prev)
                    scores.append((s_c, s_p))
            probs = []
            for s_c, s_p in scores:
                m = jnp.max(jnp.maximum(s_c, s_p), axis=-1, keepdims=True)
                p_c = jnp.exp(s_c - m)
                p_p = jnp.exp(s_p - m)
                den = jnp.sum(p_c + p_p, axis=-1, keepdims=True)
                probs.append((p_c.astype(BF16), p_p.astype(BF16), den, m))
            for u, (cur, no_prev, qn, kn, vn, kp, vp) in enumerate(blocks):
                outs, lses = [], []
                for hh in range(2):
                    p_c, p_p, den, m = probs[2 * u + hh]
                    acc = (jnp.dot(p_c, vn, preferred_element_type=F32) + jnp.dot(p_p, vp, preferred_element_type=F32))
                    outs.append(acc / den)
                    lses.append(m + jnp.log(den))
                o_scr[cur, :] = jnp.where(low, outs[0], outs[1])
                l_scr[cur, :] = jnp.where(low, lses[0], lses[1])
            return carry

        lax.fori_loop(0, nj // ATTN_UNROLL, step, 0)

    def merge(n, carry):
        rows = pl.ds(pl.multiple_of(n * ATTN_BLOCK, ATTN_BLOCK), ATTN_BLOCK)
        la, lb, lc = l1[rows, :], l2[rows, :], l3[rows, :]
        mx = jnp.maximum(jnp.maximum(la, lb), lc)
        ea, eb, ec = jnp.exp(la - mx), jnp.exp(lb - mx), jnp.exp(lc - mx)
        y = (ea * o1[rows, :] + eb * o2[rows, :] + ec * o3[rows, :]) / (ea + eb + ec)
        y_ref[rows, :] = y.astype(y_ref.dtype)
        return carry

    lax.fori_loop(0, nj, merge, 0)


def _attention(slopes, q, k, v, batch, seq):
    npairs = ATTN_WIDTH // LANES
    spec = pl.BlockSpec((seq, LANES), lambda b, hp, *_: (b, hp))
    seq_buf = pltpu.VMEM((seq, LANES), F32)
    return pl.pallas_call(
        functools.partial(_attn_body, seq=seq),
        grid_spec=pltpu.PrefetchScalarGridSpec(
            num_scalar_prefetch=1,
            grid=(batch, npairs),
            in_specs=[spec, spec, spec],
            out_specs=spec,
            scratch_shapes=[pltpu.VMEM((4, ATTN_BLOCK, ATTN_BLOCK), F32)] + [seq_buf] * 6,
        ),
        out_shape=jax.ShapeDtypeStruct((batch * seq, ATTN_WIDTH), BF16),
        compiler_params=_params(("parallel", "parallel")),
        name="attention",
    )(slopes, q, k, v)


def _route_chunk(logits, rbias, run, upper, ones):
    nj = GROUP_SIZE
    gidx = lax.broadcasted_iota(I32, (N_GROUPS, LANES), 0)
    sc = [_sigmoid(logits[j * 8:(j + 1) * 8, :]) for j in range(nj)]
    bs = [sc[j] + rbias[j * 8:(j + 1) * 8, :] for j in range(nj)]
    tree = lambda f, xs: functools.reduce(f, xs)
    neg_inf = -jnp.inf
    m1 = tree(jnp.maximum, bs)
    first = tree(jnp.minimum, [jnp.where(bs[j] == m1, j, nj) for j in range(nj)])
    m2 = tree(jnp.maximum, [jnp.where(first == j, neg_inf, bs[j]) for j in range(nj)])
    gs = m1 + m2
    beaten = jnp.zeros((N_GROUPS, LANES), I32)
    for g2 in range(N_GROUPS):
        other = gs[g2:g2 + 1, :]
        beats = (other > gs) | ((other == gs) & (g2 < gidx))
        beaten = beaten + beats.astype(I32)
    gsel = beaten < TOP_GROUPS
    mk = [jnp.where(gsel, bs[j], neg_inf) for j in range(nj)]
    eid = [gidx * GROUP_SIZE + j for j in range(nj)]
    selm = [jnp.zeros((N_GROUPS, LANES), F32) for _ in range(nj)]
    idx_rows, gate_rows = [], []
    for _ in range(TOP_K):
        mx = jnp.max(tree(jnp.maximum, mk), axis=0, keepdims=True)
        cand = tree(jnp.minimum, [jnp.where(mk[j] == mx, eid[j], N_EXPERTS) for j in range(nj)])
        sel = jnp.min(cand, axis=0, keepdims=True)
        hit = [eid[j] == sel for j in range(nj)]
        gk = jnp.sum(tree(jnp.add, [jnp.where(hit[j], sc[j], 0.0) for j in range(nj)]), axis=0, keepdims=True)
        mk = [jnp.where(hit[j], neg_inf, mk[j]) for j in range(nj)]
        selm = [selm[j] + hit[j].astype(F32) for j in range(nj)]
        idx_rows.append(sel)
        gate_rows.append(gk)
    gsum = tree(jnp.add, gate_rows)
    gate = jnp.concatenate([g / gsum * ROUTE_SCALE for g in gate_rows], axis=0)
    idx = jnp.concatenate(idx_rows, axis=0)
    msel = jnp.concatenate(selm, axis=0).astype(BF16)
    before = run + jnp.dot(msel, upper, preferred_element_type=F32)
    rank_rows = []
    for kk in range(TOP_K):
        contrib = [jnp.where(eid[j] == idx_rows[kk], before[j * 8:(j + 1) * 8, :], 0.0) for j in range(nj)]
        rank_rows.append(jnp.sum(tree(jnp.add, contrib), axis=0, keepdims=True))
    rank = jnp.concatenate(rank_rows, axis=0).astype(I32)
    run = run + jnp.dot(msel, ones, preferred_element_type=F32)
    return idx, gate, rank, run


def _post_body(yab_ref, yc_ref, x_ref, mod_ref, gc_ref, wout_ref,
               n2g_ref, rwh_ref, rwl_ref, rb_ref, sw1_ref, sw3_ref, sw2_ref,
               x2_ref, h2_ref, idx_ref, gate_ref, rank_ref, cnt_ref, run_ref, *, tm):
    i = pl.program_id(0)

    @pl.when(i == 0)
    def _():
        run_ref[...] = jnp.zeros_like(run_ref)

    yc = yc_ref[...].astype(F32)
    yc = yc * lax.rsqrt(jnp.mean(yc * yc, axis=-1, keepdims=True) + EPS) * gc_ref[...]
    nab = SGU_WIDTH + CONV_WIDTH
    mix = (jnp.dot(yab_ref[...], wout_ref[0:nab, :], preferred_element_type=F32)
           + jnp.dot(yc.astype(BF16), wout_ref[nab:, :], preferred_element_type=F32))
    x1 = x_ref[...] + mod_ref[2:3, :] * mix
    h2 = x1 * lax.rsqrt(jnp.mean(x1 * x1, axis=-1, keepdims=True) + EPS) * n2g_ref[...]
    h2 = h2 * (1.0 + mod_ref[4:5, :]) + mod_ref[3:4, :]
    for c in range(ROW_TILES):
        h2_ref[pl.ds(c, tm, stride=ROW_TILES), :] = h2[:, c * LANES:(c + 1) * LANES]
    hb = h2.astype(BF16)
    a = jnp.dot(hb, sw1_ref[...], preferred_element_type=F32)
    b = jnp.dot(hb, sw3_ref[...], preferred_element_type=F32)
    sh = jnp.dot((_silu(a) * b).astype(BF16), sw2_ref[...], preferred_element_type=F32)
    x2_ref[...] = x1 + mod_ref[5:6, :] * sh
    hl = (h2 - hb.astype(F32)).astype(BF16)
    dn = (((1,), (1,)), ((), ()))
    logits = (lax.dot_general(rwh_ref[...], hb, dn, preferred_element_type=F32)
              + lax.dot_general(rwl_ref[...], hb, dn, preferred_element_type=F32)
              + lax.dot_general(rwh_ref[...], hl, dn, preferred_element_type=F32))
    tt = lax.broadcasted_iota(I32, (LANES, LANES), 0)
    tc = lax.broadcasted_iota(I32, (LANES, LANES), 1)
    upper = (tt < tc).astype(BF16)
    ones = jnp.ones((LANES, LANES), BF16)
    run = run_ref[...]
    for c in range(tm // LANES):
        cols = slice(c * LANES, (c + 1) * LANES)
        idx, gate, rank, run = _route_chunk(logits[:, cols], rb_ref[...], run, upper, ones)
        idx_ref[:, cols] = idx
        gate_ref[:, cols] = gate
        rank_ref[:, cols] = rank
    run_ref[...] = run
    cnt_ref[...] = run


def _post(yab, yc, x2d, mod_l, g_c, wout_b, n2g, rwh, rwl, rb, sw1, sw3, sw2, seq, tm=256):
    t, d = x2d.shape
    row = lambda a: a.reshape(1, -1)
    tok = lambda w: pl.BlockSpec((tm, w), lambda i: (i, 0))
    full = lambda a: pl.BlockSpec(a.shape, lambda i: (0,) * a.ndim)
    consts = (row(g_c), wout_b, row(n2g), rwh, rwl, rb, sw1, sw3, sw2)
    kt = lambda dt: jax.ShapeDtypeStruct((TOP_K, t), dt)
    kspec = pl.BlockSpec((TOP_K, tm), lambda i: (0, i))
    return pl.pallas_call(
        functools.partial(_post_body, tm=tm),
        grid=(t // tm,),
        in_specs=[tok(512), tok(512), tok(d), pl.BlockSpec((None, ADA_CHUNKS, d), lambda i: ((i * tm) // seq, 0, 0))]
        + [full(a) for a in consts],
        out_specs=[tok(d), pl.BlockSpec((tm * ROW_TILES, LANES), lambda i: (i, 0)), kspec, kspec, kspec,
                   pl.BlockSpec((N_EXPERTS, LANES), lambda i: (0, 0))],
        out_shape=[jax.ShapeDtypeStruct((t, d), F32), jax.ShapeDtypeStruct((t * ROW_TILES, LANES), F32),
                   kt(I32), kt(F32), kt(I32), jax.ShapeDtypeStruct((N_EXPERTS, LANES), F32)],
        scratch_shapes=[pltpu.VMEM((N_EXPERTS, LANES), F32)],
        compiler_params=_params(("arbitrary",)),
        name="post",
    )(yab, yc, x2d, mod_l, *consts)


_PAD_BITS = (64, 32, 16, 8, 4, 2, 1)
BLOCK_SLABS = ROW_BLOCK * ROW_TILES


def _row_slab(ref, row):
    return ref.at[pl.ds(pl.multiple_of(row * ROW_TILES, ROW_TILES), ROW_TILES), :]


def _dispatch_body(pstart_ref, cnt_ref, nu_ref, idx_ref, rank_ref, h_ref, xd_ref, zbuf, sem, zsem, *, tq, n_blocks):
    i = pl.program_id(0)

    @pl.when(i == 0)
    def _():
        zbuf[...] = jnp.zeros_like(zbuf)

        def pad_copies(e, act):
            base = pstart_ref[e] + cnt_ref[e]
            npad = (-cnt_ref[e]) & (ROW_BLOCK - 1)
            for bit in _PAD_BITS:
                nrows = bit * ROW_TILES

                @pl.when((npad & bit) != 0)
                def _(base=base, nrows=nrows):
                    dst = xd_ref.at[pl.ds(pl.multiple_of(base * ROW_TILES, ROW_TILES), nrows), :]
                    act(pltpu.make_async_copy(zbuf.at[pl.ds(0, nrows), :], dst, zsem))

                base = base + (npad & bit)

        def tail_copy(blk, act):
            dst = xd_ref.at[pl.ds(pl.multiple_of(blk * BLOCK_SLABS, BLOCK_SLABS), BLOCK_SLABS), :]
            act(pltpu.make_async_copy(zbuf, dst, zsem))

        def run_all(act):
            def per_expert(e, carry):
                pad_copies(e, act)
                return carry

            def per_tail(blk, carry):
                tail_copy(blk, act)
                return carry

            lax.fori_loop(0, N_EXPERTS, per_expert, 0)
            lax.fori_loop(nu_ref[0], n_blocks, per_tail, 0)

        run_all(lambda cp: cp.start())
        run_all(lambda cp: cp.wait())

    def token(t, carry):
        src = _row_slab(h_ref, t)
        for kk in range(TOP_K):
            dest = pstart_ref[idx_ref[kk, t]] + rank_ref[kk, t]
            pltpu.make_async_copy(src, _row_slab(xd_ref, dest), sem).start(priority=kk % 2)
        return carry

    lax.fori_loop(0, tq, token, 0)
    for kk in range(TOP_K):
        pltpu.make_async_copy(h_ref, xd_ref.at[pl.ds(0, tq * ROW_TILES), :], sem).wait()


def _dispatch(pstart, counts, n_used, idx, rank, h2rows, n_blocks, tq=512):
    t = idx.shape[1]
    smem = pl.BlockSpec((TOP_K, tq), lambda i, *_: (0, i), memory_space=pltpu.SMEM)
    return pl.pallas_call(
        functools.partial(_dispatch_body, tq=tq, n_blocks=n_blocks),
        grid_spec=pltpu.PrefetchScalarGridSpec(
            num_scalar_prefetch=3,
            grid=(t // tq,),
            in_specs=[smem, smem, pl.BlockSpec((tq * ROW_TILES, LANES), lambda i, *_: (i, 0))],
            out_specs=pl.BlockSpec(memory_space=pl.ANY),
            scratch_shapes=[pltpu.VMEM((BLOCK_SLABS, LANES), F32),
                            pltpu.SemaphoreType.DMA, pltpu.SemaphoreType.DMA],
        ),
        out_shape=jax.ShapeDtypeStruct((n_blocks * BLOCK_SLABS, LANES), F32),
        compiler_params=_params(("arbitrary",)),
        name="dispatch",
    )(pstart, counts, n_used, idx, rank, h2rows)


def _expert_body(pstart_ref, cnt_ref, nu_ref, w1_ref, w3_ref, w2_ref, xd_ref, yd_ref,
                 w1b, w3b, w2b, xbuf, ybuf, zbuf, in_sem, out_sem, z_sem, *, n_blocks):
    e = pl.program_id(0)
    n_used = nu_ref[0]
    nblk = (cnt_ref[e] + ROW_BLOCK - 1) // ROW_BLOCK
    g0 = pstart_ref[e] // ROW_BLOCK

    def hbm_block(ref, g):
        return ref.at[pl.ds(pl.multiple_of(g * BLOCK_SLABS, BLOCK_SLABS), BLOCK_SLABS), :]

    def slot_block(ref, slot):
        return ref.at[pl.ds(pl.multiple_of(slot * BLOCK_SLABS, BLOCK_SLABS), BLOCK_SLABS), :]

    def load(g, slot):
        return pltpu.make_async_copy(hbm_block(xd_ref, g), slot_block(xbuf, slot), in_sem.at[slot])

    def store(g, slot):
        return pltpu.make_async_copy(slot_block(ybuf, slot), hbm_block(yd_ref, g), out_sem.at[slot])

    @pl.when(e == 0)
    def _():
        load(0, 0).start()

    @pl.when(nblk > 0)
    def _():
        w1b[...] = w1_ref[...].astype(BF16)
        w3b[...] = w3_ref[...].astype(BF16)
        w2b[...] = w2_ref[...].astype(BF16)

        def body(b, carry):
            g = g0 + b
            slot = lax.rem(g, 2)

            @pl.when(g + 1 < n_used)
            def _():
                load(g + 1, 1 - slot).start()

            load(g, slot).wait()
            base = slot * BLOCK_SLABS
            x = jnp.concatenate(
                [xbuf[pl.ds(base + c, ROW_BLOCK, stride=ROW_TILES), :] for c in range(ROW_TILES)], axis=1)
            xb = x.astype(BF16)
            a = jnp.dot(xb, w1b[...], preferred_element_type=F32)
            bb = jnp.dot(xb, w3b[...], preferred_element_type=F32)
            y = jnp.dot((_silu(a) * bb).astype(BF16), w2b[...], preferred_element_type=F32)

            @pl.when(g >= 2)
            def _():
                store(g - 2, slot).wait()

            for c in range(ROW_TILES):
                ybuf[pl.ds(base + c, ROW_BLOCK, stride=ROW_TILES), :] = y[:, c * LANES:(c + 1) * LANES]
            store(g, slot).start()
            return carry

        lax.fori_loop(0, nblk, body, 0)

    @pl.when(e == pl.num_programs(0) - 1)
    def _():
        @pl.when(n_used >= 2)
        def _():
            store(n_used - 2, lax.rem(n_used, 2)).wait()

        store(n_used - 1, lax.rem(n_used - 1, 2)).wait()
        zbuf[...] = jnp.zeros_like(zbuf)

        def tail(act):
            def per_block(g, carry):
                act(pltpu.make_async_copy(zbuf, hbm_block(yd_ref, g), z_sem))
                return carry

            lax.fori_loop(n_used, n_blocks, per_block, 0)

        tail(lambda cp: cp.start())
        tail(lambda cp: cp.wait())


def _experts(pstart, counts, n_used, xd, w1, w3, w2, layer, n_blocks):
    d, ff = w1.shape[2], w1.shape[3]
    wspec = lambda a, b: pl.BlockSpec((None, None, a, b), lambda e, *_: (layer, e, 0, 0))
    ring = pltpu.VMEM((2 * BLOCK_SLABS, LANES), F32)
    return pl.pallas_call(
        functools.partial(_expert_body, n_blocks=n_blocks),
        grid_spec=pltpu.PrefetchScalarGridSpec(
            num_scalar_prefetch=3,
            grid=(N_EXPERTS,),
            in_specs=[wspec(d, ff), wspec(d, ff), wspec(ff, d), pl.BlockSpec(memory_space=pl.ANY)],
            out_specs=pl.BlockSpec(memory_space=pl.ANY),
            scratch_shapes=[pltpu.VMEM((d, ff), BF16), pltpu.VMEM((d, ff), BF16), pltpu.VMEM((ff, d), BF16),
                            ring, ring, pltpu.VMEM((BLOCK_SLABS, LANES), F32),
                            pltpu.SemaphoreType.DMA((2,)), pltpu.SemaphoreType.DMA((2,)), pltpu.SemaphoreType.DMA],
        ),
        out_shape=jax.ShapeDtypeStruct(xd.shape, F32),
        compiler_params=_params(("arbitrary",)),
        name="experts",
    )(pstart, counts, n_used, w1, w3, w2, xd)


def _combine_body(pstart_ref, idx_ref, rank_ref, idxn_ref, rankn_ref, gt_ref, x2_ref, mod_ref, fg_ref, yd_ref,
                  o_ref, buf, sems, *, tq, final):
    i = pl.program_id(0)
    tile_rows = tq * ROW_TILES
    slot_rows = TOP_K * tile_rows
    slot = lax.rem(i, 2)

    def issue(idx_r, rank_r, into):
        def token(t, carry):
            for kk in range(TOP_K):
                dest = pstart_ref[idx_r[kk, t]] + rank_r[kk, t]
                row0 = pl.multiple_of(into * slot_rows + kk * tile_rows + t * ROW_TILES, ROW_TILES)
                pltpu.make_async_copy(_row_slab(yd_ref, dest), buf.at[pl.ds(row0, ROW_TILES), :],
                                      sems.at[into]).start(priority=kk % 2)
            return carry

        lax.fori_loop(0, tq, token, 0)

    @pl.when(i == 0)
    def _():
        issue(idx_ref, rank_ref, 0)

    @pl.when(i + 1 < pl.num_programs(0))
    def _():
        issue(idxn_ref, rankn_ref, 1 - slot)

    base = slot * slot_rows
    for kk in range(TOP_K):
        row0 = pl.multiple_of(base + kk * tile_rows, ROW_TILES)
        pltpu.make_async_copy(yd_ref.at[pl.ds(0, tile_rows), :], buf.at[pl.ds(row0, tile_rows), :],
                              sems.at[slot]).wait()
    gates = gt_ref[...]
    sq = jnp.zeros((tq, 1), F32)
    for c in range(ROW_TILES):
        routed = jnp.zeros((tq, LANES), F32)
        for kk in range(TOP_K):
            routed = routed + gates[:, kk:kk + 1] * buf[pl.ds(base + kk * tile_rows + c, tq, stride=ROW_TILES), :]
        cols = slice(c * LANES, (c + 1) * LANES)
        out = x2_ref[:, cols] + mod_ref[5:6, cols] * routed
        o_ref[:, cols] = out
        if final:
            sq = sq + jnp.sum(out * out, axis=-1, keepdims=True)
    if final:
        inv = lax.rsqrt(sq / D_MODEL + EPS)
        for c in range(ROW_TILES):
            cols = slice(c * LANES, (c + 1) * LANES)
            o_ref[:, cols] = o_ref[:, cols] * inv * fg_ref[:, cols]


def _combine(pstart, idx, rank, gate_t, x2, mod_l, final_g, yd, seq, final, tq=256):
    t, d = x2.shape
    nsteps = t // tq
    smem = pl.BlockSpec((TOP_K, tq), lambda i, *_: (0, i), memory_space=pltpu.SMEM)
    smem_next = pl.BlockSpec((TOP_K, tq), lambda i, *_: (0, jnp.minimum(i + 1, nsteps - 1)), memory_space=pltpu.SMEM)
    return pl.pallas_call(
        functools.partial(_combine_body, tq=tq, final=final),
        grid_spec=pltpu.PrefetchScalarGridSpec(
            num_scalar_prefetch=1,
            grid=(nsteps,),
            in_specs=[smem, smem, smem_next, smem_next,
                      pl.BlockSpec((tq, TOP_K), lambda i, *_: (i, 0)),
                      pl.BlockSpec((tq, d), lambda i, *_: (i, 0)),
                      pl.BlockSpec((None, ADA_CHUNKS, d), lambda i, *_: ((i * tq) // seq, 0, 0)),
                      pl.BlockSpec((1, d), lambda i, *_: (0, 0)),
                      pl.BlockSpec(memory_space=pl.ANY)],
            out_specs=pl.BlockSpec((tq, d), lambda i, *_: (i, 0)),
            scratch_shapes=[pltpu.VMEM((2 * TOP_K * tq * ROW_TILES, LANES), F32), pltpu.SemaphoreType.DMA((2,))],
        ),
        out_shape=jax.ShapeDtypeStruct((t, d), F32),
        compiler_params=_params(("arbitrary",)),
        name="combine",
    )(pstart, idx, rank, idx, rank, gate_t, x2, mod_l, final_g.reshape(1, d), yd)


def _row_layout(counts_perm):
    counts = counts_perm.reshape(GROUP_SIZE, N_GROUPS).T.reshape(N_EXPERTS).astype(I32)
    padded = (counts + ROW_BLOCK - 1) // ROW_BLOCK * ROW_BLOCK
    pend = jnp.cumsum(padded)
    pstart = pend - padded
    n_used = (pend[-1:] // ROW_BLOCK).astype(I32)
    return counts, pstart.astype(I32), n_used


def _perm_rows(a):
    return a.reshape((N_GROUPS, GROUP_SIZE) + a.shape[1:]).swapaxes(0, 1).reshape(a.shape)


def kernel(x, c, norm1_g, norm2_g, ada_w, ada_b, w_in, sgu_norm_g, sgu_w, sgu_b, conv_w, conv_b, conv_ln_g,
           conv_ln_b, mix_norm_g, w_out, router_w, router_bias, expert_w1, expert_w3, expert_w2, shared_w1,
           shared_w3, shared_w2, final_norm_g):
    batch, seq, d = x.shape
    depth = ada_w.shape[0]
    t = batch * seq
    assert d == D_MODEL and seq % (DILATIONS[-1] * ATTN_BLOCK) == 0 and t % 512 == 0
    assert (seq // ATTN_BLOCK) % ATTN_UNROLL == 0
    n_blocks = (t * TOP_K + N_EXPERTS * (ROW_BLOCK - 1) + ROW_BLOCK - 1) // ROW_BLOCK

    slopes = jnp.exp2(-ALIBI_MAX * jnp.arange(1, ATTN_HEADS + 1, dtype=F32) / ATTN_HEADS)
    c_pad = jnp.zeros((16, d), F32).at[:batch].set(c)
    mod = _ada_mod(c_pad, ada_w, ada_b)[:, :batch].reshape(depth, batch, ADA_CHUNKS, d)

    xc = x.reshape(t, d)
    nab = SGU_WIDTH + CONV_WIDTH
    for l in range(depth):
        za, zb, q, k, v = _inproj(xc, mod[l], norm1_g[l], w_in[l].astype(BF16), seq)
        wcat = jnp.concatenate([sgu_w[l, h] for h in range(SGU_HEADS)], axis=1)
        bmat = jnp.repeat(sgu_b[l].T, HEAD_DIM, axis=1)
        yab = _sguconv(za, zb, sgu_norm_g[l], wcat, bmat, conv_w[l], conv_b[l], conv_ln_g[l], conv_ln_b[l],
                       mix_norm_g[l, :SGU_WIDTH], mix_norm_g[l, SGU_WIDTH:nab], batch, seq)
        yc = _attention(slopes, q, k, v, batch, seq)
        rwt = _perm_rows(router_w[l].T)
        rwh = rwt.astype(BF16)
        rwl = (rwt - rwh.astype(F32)).astype(BF16)
        rb = _perm_rows(router_bias[l]).reshape(N_EXPERTS, 1)
        x2, h2rows, idx, gate, rank, cnt = _post(
            yab, yc, xc, mod[l], mix_norm_g[l, nab:], w_out[l].astype(BF16), norm2_g[l], rwh, rwl, rb,
            shared_w1[l].astype(BF16), shared_w3[l].astype(BF16), shared_w2[l].astype(BF16), seq)
        counts, pstart, n_used = _row_layout(cnt[:, 0])
        xd = _dispatch(pstart, counts, n_used, idx, rank, h2rows, n_blocks)
        yd = _experts(pstart, counts, n_used, xd, expert_w1, expert_w3, expert_w2, l, n_blocks)
        xc = _combine(pstart, idx, rank, gate.T, x2, mod[l], final_norm_g, yd, seq, final=(l == depth - 1))
    return xc.reshape(batch, seq, d)
```

```python
import functools

import jax
import jax.numpy as jnp
from jax import lax
from jax.experimental import pallas as pl
from jax.experimental.pallas import tpu as pltpu

F32 = jnp.float32
BF16 = jnp.bfloat16
I32 = jnp.int32

D_MODEL = 1024
HEAD_DIM = 64
SGU_WIDTH = 256
CONV_WIDTH = 256
ATTN_WIDTH = 512
ATTN_HEADS = 8
SGU_HEADS = 4
SGU_CHUNK = 128
CONV_KERNEL = 31
CONV_HALO = 32
DILATIONS = (1, 4, 16)
ATTN_BLOCK = 128
ATTN_UNROLL = 4
ALIBI_MAX = 8.0
N_EXPERTS = 256
N_GROUPS = 8
GROUP_SIZE = N_EXPERTS // N_GROUPS
TOP_K = 8
TOP_GROUPS = 4
EXPERT_FF = 256
ROUTE_SCALE = 2.5
ROW_BLOCK = 128
ADA_CHUNKS = 6
EPS = 1e-6
LANES = 128
SUBLANES = 8
ROW_TILES = D_MODEL // LANES
NEG_BIG = -1e30
VMEM_LIMIT = 56 * 1024 * 1024


def _sigmoid(x):
    return 1.0 / (1.0 + jnp.exp(-x))


def _silu(x):
    return x * _sigmoid(x)


def _params(semantics):
    return pltpu.CompilerParams(dimension_semantics=semantics, vmem_limit_bytes=VMEM_LIMIT)


def _ada_body(c_ref, w_ref, b_ref, o_ref):
    c = c_ref[...]
    cond = _silu(c).astype(BF16)
    o_ref[...] = jnp.dot(cond, w_ref[...].astype(BF16), preferred_element_type=F32) + b_ref[...]


def _ada_mod(c_pad, ada_w, ada_b, tn=1536):
    depth, d, n = ada_w.shape
    rows = c_pad.shape[0]
    return pl.pallas_call(
        _ada_body,
        grid=(depth, n // tn),
        in_specs=[
            pl.BlockSpec((rows, d), lambda l, j: (0, 0)),
            pl.BlockSpec((None, d, tn), lambda l, j: (l, 0, j)),
            pl.BlockSpec((None, 1, tn), lambda l, j: (l, 0, j)),
        ],
        out_specs=pl.BlockSpec((None, rows, tn), lambda l, j: (l, 0, j)),
        out_shape=jax.ShapeDtypeStruct((depth, rows, n), F32),
        compiler_params=_params(("parallel", "parallel")),
        name="ada_mod",
    )(c_pad, ada_w, ada_b.reshape(depth, 1, n))


def _inproj_body(x_ref, mod_ref, g_ref, w_ref, za_ref, zb_ref, q_ref, k_ref, v_ref):
    x = x_ref[...]
    ms = jnp.mean(x * x, axis=-1, keepdims=True)
    h = x * lax.rsqrt(ms + EPS) * g_ref[...]
    h = h * (1.0 + mod_ref[1:2, :]) + mod_ref[0:1, :]
    hb = h.astype(BF16)
    outs = (za_ref, zb_ref, q_ref, k_ref, v_ref)
    for j, o_ref in enumerate(outs):
        r = jnp.dot(hb, w_ref[:, j * 512:(j + 1) * 512], preferred_element_type=F32)
        if j == 2:
            r = r * (HEAD_DIM ** -0.5)
        o_ref[...] = r.astype(o_ref.dtype)


def _inproj(x2d, mod_l, g, w_in_b, seq, tm=512):
    t, d = x2d.shape
    zout = jax.ShapeDtypeStruct((t, 512), BF16)
    aout = jax.ShapeDtypeStruct((t, 512), F32)
    ospec = pl.BlockSpec((tm, 512), lambda i: (i, 0))
    return pl.pallas_call(
        _inproj_body,
        grid=(t // tm,),
        in_specs=[
            pl.BlockSpec((tm, d), lambda i: (i, 0)),
            pl.BlockSpec((None, ADA_CHUNKS, d), lambda i: ((i * tm) // seq, 0, 0)),
            pl.BlockSpec((1, d), lambda i: (0, 0)),
            pl.BlockSpec(w_in_b.shape, lambda i: (0, 0)),
        ],
        out_specs=[ospec] * 5,
        out_shape=[zout, zout, aout, aout, aout],
        compiler_params=_params(("parallel",)),
        name="inproj",
    )(x2d, mod_l, g.reshape(1, d), w_in_b)


def _sguconv_body(za_ref, zb_ref, zbh_ref, sgug_ref, wcat_ref, bmat_ref, cw_ref, cb_ref,
                  lng_ref, lnb_ref, ga_ref, gb_ref, o_ref, ybuf, *, ts):
    i = pl.program_id(1)
    z = za_ref[...].astype(F32)
    z = 0.5 * z * (1.0 + lax.erf(z * (2.0 ** -0.5)))
    u = z[:, :SGU_WIDTH]
    v = z[:, SGU_WIDTH:]
    v = v * lax.rsqrt(jnp.mean(v * v, axis=-1, keepdims=True) + EPS) * sgug_ref[...]
    vb = v.astype(BF16)
    wt = lax.broadcasted_iota(I32, (SGU_CHUNK, SGU_HEADS * SGU_CHUNK), 0)
    ws = lax.broadcasted_iota(I32, (SGU_CHUNK, SGU_HEADS * SGU_CHUNK), 1) % SGU_CHUNK
    wb = jnp.where(ws <= wt, wcat_ref[...], 0.0).astype(BF16)
    head_of_lane = lax.broadcasted_iota(I32, (1, SGU_WIDTH), 1) // HEAD_DIM
    for c in range(ts // SGU_CHUNK):
        rows = slice(c * SGU_CHUNK, (c + 1) * SGU_CHUNK)
        vc = vb[rows, :]
        rhs = jnp.concatenate(
            [jnp.where(head_of_lane == h, vc, jnp.zeros_like(vc)) for h in range(SGU_HEADS)], axis=0)
        mixed = jnp.dot(wb, rhs, preferred_element_type=F32) + bmat_ref[...]
        ya = u[rows, :] * mixed
        ya = ya * lax.rsqrt(jnp.mean(ya * ya, axis=-1, keepdims=True) + EPS) * ga_ref[...]
        o_ref[rows, 0:SGU_WIDTH] = ya.astype(o_ref.dtype)
    zh = zbh_ref[...].astype(F32)
    yh = zh[:, :CONV_WIDTH] * _sigmoid(zh[:, CONV_WIDTH:])
    ybuf[0:CONV_HALO, :] = jnp.where(i == 0, 0.0, yh)
    zc = zb_ref[...].astype(F32)
    ybuf[CONV_HALO:CONV_HALO + ts, :] = zc[:, :CONV_WIDTH] * _sigmoid(zc[:, CONV_WIDTH:])
    rb = 64
    first_tap = CONV_HALO - (CONV_KERNEL - 1)
    for blk in range(ts // rb):
        acc = jnp.zeros((rb, CONV_WIDTH), F32) + cb_ref[...]
        for kk in range(CONV_KERNEL):
            start = blk * rb + first_tap + kk
            acc = acc + cw_ref[kk:kk + 1, :] * ybuf[start:start + rb, :]
        mu = jnp.mean(acc, axis=-1, keepdims=True)
        cen = acc - mu
        var = jnp.mean(cen * cen, axis=-1, keepdims=True)
        y = cen * lax.rsqrt(var + EPS) * lng_ref[...] + lnb_ref[...]
        y = _silu(y)
        y = y * lax.rsqrt(jnp.mean(y * y, axis=-1, keepdims=True) + EPS) * gb_ref[...]
        o_ref[blk * rb:(blk + 1) * rb, SGU_WIDTH:SGU_WIDTH + CONV_WIDTH] = y.astype(o_ref.dtype)


def _sguconv(za, zb, sgu_g, wcat, bmat, conv_w, conv_b, ln_g, ln_b, g_a, g_b, batch, seq, ts=512):
    t = za.shape[0]
    nt = seq // ts
    hpt = ts // CONV_HALO
    row = lambda a: a.reshape(1, -1)
    full = lambda a: pl.BlockSpec(a.shape, lambda b, i: (0,) * a.ndim)
    args = (row(sgu_g), wcat, bmat, conv_w, row(conv_b), row(ln_g), row(ln_b), row(g_a), row(g_b))
    return pl.pallas_call(
        functools.partial(_sguconv_body, ts=ts),
        grid=(batch, nt),
        in_specs=[
            pl.BlockSpec((ts, 512), lambda b, i: (b * nt + i, 0)),
            pl.BlockSpec((ts, 512), lambda b, i: (b * nt + i, 0)),
            pl.BlockSpec((CONV_HALO, 512), lambda b, i: (jnp.maximum((b * nt + i) * hpt - 1, 0), 0)),
        ] + [full(a) for a in args],
        out_specs=pl.BlockSpec((ts, 512), lambda b, i: (b * nt + i, 0)),
        out_shape=jax.ShapeDtypeStruct((t, 512), BF16),
        scratch_shapes=[pltpu.VMEM((CONV_HALO + ts, CONV_WIDTH), F32)],
        compiler_params=_params(("parallel", "arbitrary")),
        name="sguconv",
    )(za, zb, zb, *args)


def _attn_body(slopes_ref, q_ref, k_ref, v_ref, y_ref, bias_ref, o1, l1, o2, l2, o3, l3, *, seq):
    hp = pl.program_id(1)
    nj = seq // ATTN_BLOCK
    qi = lax.broadcasted_iota(I32, (ATTN_BLOCK, ATTN_BLOCK), 0)
    kj = lax.broadcasted_iota(I32, (ATTN_BLOCK, ATTN_BLOCK), 1)
    d_cur = qi - kj
    d_prev = d_cur + ATTN_BLOCK
    lane = lax.broadcasted_iota(I32, (1, LANES), 1)
    low = lane < HEAD_DIM
    dn = (((1,), (1,)), ((), ()))
    scratch = ((o1, l1), (o2, l2), (o3, l3))

    for (o_scr, l_scr), r in zip(scratch, DILATIONS):
        nb = nj // r
        for hh in range(2):
            slope = slopes_ref[hp * 2 + hh]
            bias_ref[2 * hh] = jnp.where(d_cur >= 0, -slope * (d_cur * r).astype(F32), NEG_BIG)
            bias_ref[2 * hh + 1] = jnp.where(d_cur <= 0, -slope * (d_prev * r).astype(F32), NEG_BIG)

        def rows_at(rho, n, r=r):
            start = rho + (r * ATTN_BLOCK) * n
            if r == 1:
                return pl.ds(pl.multiple_of(start, ATTN_BLOCK), ATTN_BLOCK)
            return pl.ds(start, ATTN_BLOCK, stride=r)

        def step(it, carry, nb=nb, rows_at=rows_at, o_scr=o_scr, l_scr=l_scr):
            blocks = []
            for u in range(ATTN_UNROLL):
                j = it * ATTN_UNROLL + u
                rho = lax.div(j, nb)
                n = lax.rem(j, nb)
                cur = rows_at(rho, n)
                prv = rows_at(rho, jnp.maximum(n - 1, 0))
                no_prev = jnp.where(n == 0, NEG_BIG, 0.0)
                blocks.append((cur, no_prev, q_ref[cur, :].astype(BF16), k_ref[cur, :].astype(BF16),
                               v_ref[cur, :].astype(BF16), k_ref[prv, :].astype(BF16), v_ref[prv, :].astype(BF16)))
            scores = []
            for cur, no_prev, qn, kn, vn, kp, vp in blocks:
                for hh in range(2):
                    mine = low if hh == 0 else jnp.logical_not(low)
                    qm = jnp.where(mine, qn, jnp.zeros_like(qn))
                    s_c = lax.dot_general(qm, kn, dn, preferred_element_type=F32) + bias_ref[2 * hh]
                    s_p = lax.dot_general(qm, kp, dn, preferred_element_type=F32) + (bias_ref[2 * hh + 1] + no_prev)
                    scores.append((s_c, s_p))
            probs = []
            for s_c, s_p in scores:
                m = jnp.max(jnp.maximum(s_c, s_p), axis=-1, keepdims=True)
                p_c = jnp.exp(s_c - m)
                p_p = jnp.exp(s_p - m)
                den = jnp.sum(p_c + p_p, axis=-1, keepdims=True)
                probs.append((p_c.astype(BF16), p_p.astype(BF16), den, m))
            for u, (cur, no_prev, qn, kn, vn, kp, vp) in enumerate(blocks):
                outs, lses = [], []
                for hh in range(2):
                    p_c, p_p, den, m = probs[2 * u + hh]
                    acc = (jnp.dot(p_c, vn, preferred_element_type=F32) + jnp.dot(p_p, vp, preferred_element_type=F32))
                    outs.append(acc / den)
                    lses.append(m + jnp.log(den))
                o_scr[cur, :] = jnp.where(low, outs[0], outs[1])
                l_scr[cur, :] = jnp.where(low, lses[0], lses[1])
            return carry

        lax.fori_loop(0, nj // ATTN_UNROLL, step, 0)

    def merge(n, carry):
        rows = pl.ds(pl.multiple_of(n * ATTN_BLOCK, ATTN_BLOCK), ATTN_BLOCK)
        la, lb, lc = l1[rows, :], l2[rows, :], l3[rows, :]
        mx = jnp.maximum(jnp.maximum(la, lb), lc)
        ea, eb, ec = jnp.exp(la - mx), jnp.exp(lb - mx), jnp.exp(lc - mx)
        y = (ea * o1[rows, :] + eb * o2[rows, :] + ec * o3[rows, :]) / (ea + eb + ec)
        y_ref[rows, :] = y.astype(y_ref.dtype)
        return carry

    lax.fori_loop(0, nj, merge, 0)


def _attention(slopes, q, k, v, batch, seq):
    npairs = ATTN_WIDTH // LANES
    spec = pl.BlockSpec((seq, LANES), lambda b, hp, *_: (b, hp))
    seq_buf = pltpu.VMEM((seq, LANES), F32)
    return pl.pallas_call(
        functools.partial(_attn_body, seq=seq),
        grid_spec=pltpu.PrefetchScalarGridSpec(
            num_scalar_prefetch=1,
            grid=(batch, npairs),
            in_specs=[spec, spec, spec],
            out_specs=spec,
            scratch_shapes=[pltpu.VMEM((4, ATTN_BLOCK, ATTN_BLOCK), F32)] + [seq_buf] * 6,
        ),
        out_shape=jax.ShapeDtypeStruct((batch * seq, ATTN_WIDTH), BF16),
        compiler_params=_params(("parallel", "parallel")),
        name="attention",
    )(slopes, q, k, v)


def _route_chunk(logits, rbias, run, upper, ones):
    nj = GROUP_SIZE
    gidx = lax.broadcasted_iota(I32, (N_GROUPS, LANES), 0)
    sc = [_sigmoid(logits[j * 8:(j + 1) * 8, :]) for j in range(nj)]
    bs = [sc[j] + rbias[j * 8:(j + 1) * 8, :] for j in range(nj)]
    tree = lambda f, xs: functools.reduce(f, xs)
    neg_inf = -jnp.inf
    m1 = tree(jnp.maximum, bs)
    first = tree(jnp.minimum, [jnp.where(bs[j] == m1, j, nj) for j in range(nj)])
    m2 = tree(jnp.maximum, [jnp.where(first == j, neg_inf, bs[j]) for j in range(nj)])
    gs = m1 + m2
    beaten = jnp.zeros((N_GROUPS, LANES), I32)
    for g2 in range(N_GROUPS):
        other = gs[g2:g2 + 1, :]
        beats = (other > gs) | ((other == gs) & (g2 < gidx))
        beaten = beaten + beats.astype(I32)
    gsel = beaten < TOP_GROUPS
    mk = [jnp.where(gsel, bs[j], neg_inf) for j in range(nj)]
    eid = [gidx * GROUP_SIZE + j for j in range(nj)]
    selm = [jnp.zeros((N_GROUPS, LANES), F32) for _ in range(nj)]
    idx_rows, gate_rows = [], []
    for _ in range(TOP_K):
        mx = jnp.max(tree(jnp.maximum, mk), axis=0, keepdims=True)
        cand = tree(jnp.minimum, [jnp.where(mk[j] == mx, eid[j], N_EXPERTS) for j in range(nj)])
        sel = jnp.min(cand, axis=0, keepdims=True)
        hit = [eid[j] == sel for j in range(nj)]
        gk = jnp.sum(tree(jnp.add, [jnp.where(hit[j], sc[j], 0.0) for j in range(nj)]), axis=0, keepdims=True)
        mk = [jnp.where(hit[j], neg_inf, mk[j]) for j in range(nj)]
        selm = [selm[j] + hit[j].astype(F32) for j in range(nj)]
        idx_rows.append(sel)
        gate_rows.append(gk)
    gsum = tree(jnp.add, gate_rows)
    gate = jnp.concatenate([g / gsum * ROUTE_SCALE for g in gate_rows], axis=0)
    idx = jnp.concatenate(idx_rows, axis=0)
    msel = jnp.concatenate(selm, axis=0).astype(BF16)
    before = run + jnp.dot(msel, upper, preferred_element_type=F32)
    rank_rows = []
    for kk in range(TOP_K):
        contrib = [jnp.where(eid[j] == idx_rows[kk], before[j * 8:(j + 1) * 8, :], 0.0) for j in range(nj)]
        rank_rows.append(jnp.sum(tree(jnp.add, contrib), axis=0, keepdims=True))
    rank = jnp.concatenate(rank_rows, axis=0).astype(I32)
    run = run + jnp.dot(msel, ones, preferred_element_type=F32)
    return idx, gate, rank, run


def _post_body(yab_ref, yc_ref, x_ref, mod_ref, gc_ref, wout_ref,
               n2g_ref, rwh_ref, rwl_ref, rb_ref, sw1_ref, sw3_ref, sw2_ref,
               x2_ref, h2_ref, idx_ref, gate_ref, rank_ref, cnt_ref, run_ref, *, tm):
    i = pl.program_id(0)

    @pl.when(i == 0)
    def _():
        run_ref[...] = jnp.zeros_like(run_ref)

    yc = yc_ref[...].astype(F32)
    yc = yc * lax.rsqrt(jnp.mean(yc * yc, axis=-1, keepdims=True) + EPS) * gc_ref[...]
    nab = SGU_WIDTH + CONV_WIDTH
    mix = (jnp.dot(yab_ref[...], wout_ref[0:nab, :], preferred_element_type=F32)
           + jnp.dot(yc.astype(BF16), wout_ref[nab:, :], preferred_element_type=F32))
    x1 = x_ref[...] + mod_ref[2:3, :] * mix
    h2 = x1 * lax.rsqrt(jnp.mean(x1 * x1, axis=-1, keepdims=True) + EPS) * n2g_ref[...]
    h2 = h2 * (1.0 + mod_ref[4:5, :]) + mod_ref[3:4, :]
    for c in range(ROW_TILES):
        h2_ref[pl.ds(c, tm, stride=ROW_TILES), :] = h2[:, c * LANES:(c + 1) * LANES]
    hb = h2.astype(BF16)
    a = jnp.dot(hb, sw1_ref[...], preferred_element_type=F32)
    b = jnp.dot(hb, sw3_ref[...], preferred_element_type=F32)
    sh = jnp.dot((_silu(a) * b).astype(BF16), sw2_ref[...], preferred_element_type=F32)
    x2_ref[...] = x1 + mod_ref[5:6, :] * sh
    hl = (h2 - hb.astype(F32)).astype(BF16)
    dn = (((1,), (1,)), ((), ()))
    logits = (lax.dot_general(rwh_ref[...], hb, dn, preferred_element_type=F32)
              + lax.dot_general(rwl_ref[...], hb, dn, preferred_element_type=F32)
              + lax.dot_general(rwh_ref[...], hl, dn, preferred_element_type=F32))
    tt = lax.broadcasted_iota(I32, (LANES, LANES), 0)
    tc = lax.broadcasted_iota(I32, (LANES, LANES), 1)
    upper = (tt < tc).astype(BF16)
    ones = jnp.ones((LANES, LANES), BF16)
    run = run_ref[...]
    for c in range(tm // LANES):
        cols = slice(c * LANES, (c + 1) * LANES)
        idx, gate, rank, run = _route_chunk(logits[:, cols], rb_ref[...], run, upper, ones)
        idx_ref[:, cols] = idx
        gate_ref[:, cols] = gate
        rank_ref[:, cols] = rank
    run_ref[...] = run
    cnt_ref[...] = run


def _post(yab, yc, x2d, mod_l, g_c, wout_b, n2g, rwh, rwl, rb, sw1, sw3, sw2, seq, tm=256):
    t, d = x2d.shape
    row = lambda a: a.reshape(1, -1)
    tok = lambda w: pl.BlockSpec((tm, w), lambda i: (i, 0))
    full = lambda a: pl.BlockSpec(a.shape, lambda i: (0,) * a.ndim)
    consts = (row(g_c), wout_b, row(n2g), rwh, rwl, rb, sw1, sw3, sw2)
    kt = lambda dt: jax.ShapeDtypeStruct((TOP_K, t), dt)
    kspec = pl.BlockSpec((TOP_K, tm), lambda i: (0, i))
    return pl.pallas_call(
        functools.partial(_post_body, tm=tm),
        grid=(t // tm,),
        in_specs=[tok(512), tok(512), tok(d), pl.BlockSpec((None, ADA_CHUNKS, d), lambda i: ((i * tm) // seq, 0, 0))]
        + [full(a) for a in consts],
        out_specs=[tok(d), pl.BlockSpec((tm * ROW_TILES, LANES), lambda i: (i, 0)), kspec, kspec, kspec,
                   pl.BlockSpec((N_EXPERTS, LANES), lambda i: (0, 0))],
        out_shape=[jax.ShapeDtypeStruct((t, d), F32), jax.ShapeDtypeStruct((t * ROW_TILES, LANES), F32),
                   kt(I32), kt(F32), kt(I32), jax.ShapeDtypeStruct((N_EXPERTS, LANES), F32)],
        scratch_shapes=[pltpu.VMEM((N_EXPERTS, LANES), F32)],
        compiler_params=_params(("arbitrary",)),
        name="post",
    )(yab, yc, x2d, mod_l, *consts)


_PAD_BITS = (64, 32, 16, 8, 4, 2, 1)
EXPERT_IN_SLOTS = 4
EXPERT_OUT_SLOTS = 3
BLOCK_SLABS = ROW_BLOCK * ROW_TILES


def _row_slab(ref, row):
    return ref.at[pl.ds(pl.multiple_of(row * ROW_TILES, ROW_TILES), ROW_TILES), :]


def _dispatch_body(pstart_ref, cnt_ref, nu_ref, idx_ref, rank_ref, h_ref, xd_ref, zbuf, sem, zsem, *, tq, n_blocks):
    i = pl.program_id(0)

    @pl.when(i == 0)
    def _():
        zbuf[...] = jnp.zeros_like(zbuf)

        def pad_copies(e, act):
            base = pstart_ref[e] + cnt_ref[e]
            npad = (-cnt_ref[e]) & (ROW_BLOCK - 1)
            for bit in _PAD_BITS:
                nrows = bit * ROW_TILES

                @pl.when((npad & bit) != 0)
                def _(base=base, nrows=nrows):
                    dst = xd_ref.at[pl.ds(pl.multiple_of(base * ROW_TILES, ROW_TILES), nrows), :]
                    act(pltpu.make_async_copy(zbuf.at[pl.ds(0, nrows), :], dst, zsem))

                base = base + (npad & bit)

        def tail_copy(blk, act):
            dst = xd_ref.at[pl.ds(pl.multiple_of(blk * BLOCK_SLABS, BLOCK_SLABS), BLOCK_SLABS), :]
            act(pltpu.make_async_copy(zbuf, dst, zsem))

        def run_all(act):
            def per_expert(e, carry):
                pad_copies(e, act)
                return carry

            def per_tail(blk, carry):
                tail_copy(blk, act)
                return carry

            lax.fori_loop(0, N_EXPERTS, per_expert, 0)
            lax.fori_loop(nu_ref[0], n_blocks, per_tail, 0)

        run_all(lambda cp: cp.start())
        run_all(lambda cp: cp.wait())

    def token(t, carry):
        src = _row_slab(h_ref, t)
        for kk in range(TOP_K):
            dest = pstart_ref[idx_ref[kk, t]] + rank_ref[kk, t]
            pltpu.make_async_copy(src, _row_slab(xd_ref, dest), sem).start(priority=kk % 2)
        return carry

    lax.fori_loop(0, tq, token, 0)
    for kk in range(TOP_K):
        pltpu.make_async_copy(h_ref, xd_ref.at[pl.ds(0, tq * ROW_TILES), :], sem).wait()


def _dispatch(pstart, counts, n_used, idx, rank, h2rows, n_blocks, tq=512):
    t = idx.shape[1]
    smem = pl.BlockSpec((TOP_K, tq), lambda i, *_: (0, i), memory_space=pltpu.SMEM)
    return pl.pallas_call(
        functools.partial(_dispatch_body, tq=tq, n_blocks=n_blocks),
        grid_spec=pltpu.PrefetchScalarGridSpec(
            num_scalar_prefetch=3,
            grid=(t // tq,),
            in_specs=[smem, smem, pl.BlockSpec((tq * ROW_TILES, LANES), lambda i, *_: (i, 0))],
            out_specs=pl.BlockSpec(memory_space=pl.ANY),
            scratch_shapes=[pltpu.VMEM((BLOCK_SLABS, LANES), F32),
                            pltpu.SemaphoreType.DMA, pltpu.SemaphoreType.DMA],
        ),
        out_shape=jax.ShapeDtypeStruct((n_blocks * BLOCK_SLABS, LANES), F32),
        compiler_params=_params(("arbitrary",)),
        name="dispatch",
    )(pstart, counts, n_used, idx, rank, h2rows)


def _expert_body(pstart_ref, cnt_ref, nu_ref, w1_ref, w3_ref, w2_ref, xd_ref, yd_ref,
                 w1b, w3b, w2b, xbuf, ybuf, zbuf, in_sem, out_sem, z_sem, *, n_blocks):
    e = pl.program_id(0)
    n_used = nu_ref[0]
    nblk = (cnt_ref[e] + ROW_BLOCK - 1) // ROW_BLOCK
    g0 = pstart_ref[e] // ROW_BLOCK

    def hbm_block(ref, g):
        return ref.at[pl.ds(pl.multiple_of(g * BLOCK_SLABS, BLOCK_SLABS), BLOCK_SLABS), :]

    def slot_block(ref, slot):
        return ref.at[pl.ds(pl.multiple_of(slot * BLOCK_SLABS, BLOCK_SLABS), BLOCK_SLABS), :]

    def load(g, slot):
        return pltpu.make_async_copy(hbm_block(xd_ref, g), slot_block(xbuf, slot), in_sem.at[slot])

    def store(g, slot):
        return pltpu.make_async_copy(slot_block(ybuf, slot), hbm_block(yd_ref, g), out_sem.at[slot])

    @pl.when(e == 0)
    def _():
        for p in range(EXPERT_IN_SLOTS - 1):

            @pl.when(p < n_used)
            def _(p=p):
                load(p, p).start()

    @pl.when(nblk > 0)
    def _():
        w1b[...] = w1_ref[...].astype(BF16)
        w3b[...] = w3_ref[...].astype(BF16)
        w2b[...] = w2_ref[...].astype(BF16)

        def body(b, carry):
            g = g0 + b
            slot = lax.rem(g, EXPERT_IN_SLOTS)
            ahead = g + (EXPERT_IN_SLOTS - 1)

            @pl.when(ahead < n_used)
            def _():
                load(ahead, lax.rem(ahead, EXPERT_IN_SLOTS)).start()

            load(g, slot).wait()
            base = slot * BLOCK_SLABS
            x = jnp.concatenate(
                [xbuf[pl.ds(base + c, ROW_BLOCK, stride=ROW_TILES), :] for c in range(ROW_TILES)], axis=1)
            xb = x.astype(BF16)
            a = jnp.dot(xb, w1b[...], preferred_element_type=F32)
            bb = jnp.dot(xb, w3b[...], preferred_element_type=F32)
            y = jnp.dot((_silu(a) * bb).astype(BF16), w2b[...], preferred_element_type=F32)

            oslot = lax.rem(g, EXPERT_OUT_SLOTS)

            @pl.when(g >= EXPERT_OUT_SLOTS)
            def _():
                store(g - EXPERT_OUT_SLOTS, oslot).wait()

            obase = oslot * BLOCK_SLABS
            for c in range(ROW_TILES):
                ybuf[pl.ds(obase + c, ROW_BLOCK, stride=ROW_TILES), :] = y[:, c * LANES:(c + 1) * LANES]
            store(g, oslot).start()
            return carry

        lax.fori_loop(0, nblk, body, 0)

    @pl.when(e == pl.num_programs(0) - 1)
    def _():
        for back in range(1, EXPERT_OUT_SLOTS + 1):

            @pl.when(n_used >= back)
            def _(back=back):
                store(n_used - back, lax.rem(n_used - back, EXPERT_OUT_SLOTS)).wait()

        zbuf[...] = jnp.zeros_like(zbuf)

        def tail(act):
            def per_block(g, carry):
                act(pltpu.make_async_copy(zbuf, hbm_block(yd_ref, g), z_sem))
                return carry

            lax.fori_loop(n_used, n_blocks, per_block, 0)

        tail(lambda cp: cp.start())
        tail(lambda cp: cp.wait())


def _experts(pstart, counts, n_used, xd, w1, w3, w2, layer, n_blocks):
    d, ff = w1.shape[2], w1.shape[3]
    wspec = lambda a, b: pl.BlockSpec((None, None, a, b), lambda e, *_: (layer, e, 0, 0))
    in_ring = pltpu.VMEM((EXPERT_IN_SLOTS * BLOCK_SLABS, LANES), F32)
    out_ring = pltpu.VMEM((EXPERT_OUT_SLOTS * BLOCK_SLABS, LANES), F32)
    return pl.pallas_call(
        functools.partial(_expert_body, n_blocks=n_blocks),
        grid_spec=pltpu.PrefetchScalarGridSpec(
            num_scalar_prefetch=3,
            grid=(N_EXPERTS,),
            in_specs=[wspec(d, ff), wspec(d, ff), wspec(ff, d), pl.BlockSpec(memory_space=pl.ANY)],
            out_specs=pl.BlockSpec(memory_space=pl.ANY),
            scratch_shapes=[pltpu.VMEM((d, ff), BF16), pltpu.VMEM((d, ff), BF16), pltpu.VMEM((ff, d), BF16),
                            in_ring, out_ring, pltpu.VMEM((BLOCK_SLABS, LANES), F32),
                            pltpu.SemaphoreType.DMA((EXPERT_IN_SLOTS,)), pltpu.SemaphoreType.DMA((EXPERT_OUT_SLOTS,)),
                            pltpu.SemaphoreType.DMA],
        ),
        out_shape=jax.ShapeDtypeStruct(xd.shape, F32),
        compiler_params=_params(("arbitrary",)),
        name="experts",
    )(pstart, counts, n_used, w1, w3, w2, xd)


def _combine_body(pstart_ref, idx_ref, rank_ref, idxn_ref, rankn_ref, gt_ref, x2_ref, mod_ref, fg_ref, yd_ref,
                  o_ref, buf, sems, *, tq, final):
    i = pl.program_id(0)
    tile_rows = tq * ROW_TILES
    slot_rows = TOP_K * tile_rows
    slot = lax.rem(i, 2)

    def issue(idx_r, rank_r, into):
        def token(t, carry):
            for kk in range(TOP_K):
                dest = pstart_ref[idx_r[kk, t]] + rank_r[kk, t]
                row0 = pl.multiple_of(into * slot_rows + kk * tile_rows + t * ROW_TILES, ROW_TILES)
                pltpu.make_async_copy(_row_slab(yd_ref, dest), buf.at[pl.ds(row0, ROW_TILES), :],
                                      sems.at[into]).start(priority=kk % 2)
            return carry

        lax.fori_loop(0, tq, token, 0)

    @pl.when(i == 0)
    def _():
        issue(idx_ref, rank_ref, 0)

    @pl.when(i + 1 < pl.num_programs(0))
    def _():
        issue(idxn_ref, rankn_ref, 1 - slot)

    base = slot * slot_rows
    for kk in range(TOP_K):
        row0 = pl.multiple_of(base + kk * tile_rows, ROW_TILES)
        pltpu.make_async_copy(yd_ref.at[pl.ds(0, tile_rows), :], buf.at[pl.ds(row0, tile_rows), :],
                              sems.at[slot]).wait()
    gates = gt_ref[...]
    sq = jnp.zeros((tq, 1), F32)
    for c in range(ROW_TILES):
        routed = jnp.zeros((tq, LANES), F32)
        for kk in range(TOP_K):
            routed = routed + gates[:, kk:kk + 1] * buf[pl.ds(base + kk * tile_rows + c, tq, stride=ROW_TILES), :]
        cols = slice(c * LANES, (c + 1) * LANES)
        out = x2_ref[:, cols] + mod_ref[5:6, cols] * routed
        o_ref[:, cols] = out
        if final:
            sq = sq + jnp.sum(out * out, axis=-1, keepdims=True)
    if final:
        inv = lax.rsqrt(sq / D_MODEL + EPS)
        for c in range(ROW_TILES):
            cols = slice(c * LANES, (c + 1) * LANES)
            o_ref[:, cols] = o_ref[:, cols] * inv * fg_ref[:, cols]


def _combine(pstart, idx, rank, gate_t, x2, mod_l, final_g, yd, seq, final, tq=256):
    t, d = x2.shape
    nsteps = t // tq
    smem = pl.BlockSpec((TOP_K, tq), lambda i, *_: (0, i), memory_space=pltpu.SMEM)
    smem_next = pl.BlockSpec((TOP_K, tq), lambda i, *_: (0, jnp.minimum(i + 1, nsteps - 1)), memory_space=pltpu.SMEM)
    return pl.pallas_call(
        functools.partial(_combine_body, tq=tq, final=final),
        grid_spec=pltpu.PrefetchScalarGridSpec(
            num_scalar_prefetch=1,
            grid=(nsteps,),
            in_specs=[smem, smem, smem_next, smem_next,
                      pl.BlockSpec((tq, TOP_K), lambda i, *_: (i, 0)),
                      pl.BlockSpec((tq, d), lambda i, *_: (i, 0)),
                      pl.BlockSpec((None, ADA_CHUNKS, d), lambda i, *_: ((i * tq) // seq, 0, 0)),
                      pl.BlockSpec((1, d), lambda i, *_: (0, 0)),
                      pl.BlockSpec(memory_space=pl.ANY)],
            out_specs=pl.BlockSpec((tq, d), lambda i, *_: (i, 0)),
            scratch_shapes=[pltpu.VMEM((2 * TOP_K * tq * ROW_TILES, LANES), F32), pltpu.SemaphoreType.DMA((2,))],
        ),
        out_shape=jax.ShapeDtypeStruct((t, d), F32),
        compiler_params=_params(("arbitrary",)),
        name="combine",
    )(pstart, idx, rank, idx, rank, gate_t, x2, mod_l, final_g.reshape(1, d), yd)


def _row_layout(counts_perm):
    counts = counts_perm.reshape(GROUP_SIZE, N_GROUPS).T.reshape(N_EXPERTS).astype(I32)
    padded = (counts + ROW_BLOCK - 1) // ROW_BLOCK * ROW_BLOCK
    pend = jnp.cumsum(padded)
    pstart = pend - padded
    n_used = (pend[-1:] // ROW_BLOCK).astype(I32)
    return counts, pstart.astype(I32), n_used


def _perm_rows(a):
    return a.reshape((N_GROUPS, GROUP_SIZE) + a.shape[1:]).swapaxes(0, 1).reshape(a.shape)


def kernel(x, c, norm1_g, norm2_g, ada_w, ada_b, w_in, sgu_norm_g, sgu_w, sgu_b, conv_w, conv_b, conv_ln_g,
           conv_ln_b, mix_norm_g, w_out, router_w, router_bias, expert_w1, expert_w3, expert_w2, shared_w1,
           shared_w3, shared_w2, final_norm_g):
    batch, seq, d = x.shape
    depth = ada_w.shape[0]
    t = batch * seq
    assert d == D_MODEL and seq % (DILATIONS[-1] * ATTN_BLOCK) == 0 and t % 512 == 0
    assert (seq // ATTN_BLOCK) % ATTN_UNROLL == 0
    n_blocks = (t * TOP_K + N_EXPERTS * (ROW_BLOCK - 1) + ROW_BLOCK - 1) // ROW_BLOCK

    slopes = jnp.exp2(-ALIBI_MAX * jnp.arange(1, ATTN_HEADS + 1, dtype=F32) / ATTN_HEADS)
    c_pad = jnp.zeros((16, d), F32).at[:batch].set(c)
    mod = _ada_mod(c_pad, ada_w, ada_b)[:, :batch].reshape(depth, batch, ADA_CHUNKS, d)

    xc = x.reshape(t, d)
    nab = SGU_WIDTH + CONV_WIDTH
    for l in range(depth):
        za, zb, q, k, v = _inproj(xc, mod[l], norm1_g[l], w_in[l].astype(BF16), seq)
        wcat = jnp.concatenate([sgu_w[l, h] for h in range(SGU_HEADS)], axis=1)
        bmat = jnp.repeat(sgu_b[l].T, HEAD_DIM, axis=1)
        yab = _sguconv(za, zb, sgu_norm_g[l], wcat, bmat, conv_w[l], conv_b[l], conv_ln_g[l], conv_ln_b[l],
                       mix_norm_g[l, :SGU_WIDTH], mix_norm_g[l, SGU_WIDTH:nab], batch, seq)
        yc = _attention(slopes, q, k, v, batch, seq)
        rwt = _perm_rows(router_w[l].T)
        rwh = rwt.astype(BF16)
        rwl = (rwt - rwh.astype(F32)).astype(BF16)
        rb = _perm_rows(router_bias[l]).reshape(N_EXPERTS, 1)
        x2, h2rows, idx, gate, rank, cnt = _post(
            yab, yc, xc, mod[l], mix_norm_g[l, nab:], w_out[l].astype(BF16), norm2_g[l], rwh, rwl, rb,
            shared_w1[l].astype(BF16), shared_w3[l].astype(BF16), shared_w2[l].astype(BF16), seq)
        counts, pstart, n_used = _row_layout(cnt[:, 0])
        xd = _dispatch(pstart, counts, n_used, idx, rank, h2rows, n_blocks)
        yd = _experts(pstart, counts, n_used, xd, expert_w1, expert_w3, expert_w2, l, n_blocks)
        xc = _combine(pstart, idx, rank, gate.T, x2, mod[l], final_norm_g, yd, seq, final=(l == depth - 1))
    return xc.reshape(batch, seq, d)
```
